```python
import jax, jax.numpy as jnp
from jax import lax
import numpy as np

D_MODEL = 1024
BATCH = 32
SEQ = 256
DEPTH = 4
DEC_BATCH = 2
DEC_SEQ = 1024
PAST_LEN = 512

GRID_W = 64
SSD_D_INNER = 1024
SSD_HEAD_DIM = 64
SSD_HEADS = SSD_D_INNER // SSD_HEAD_DIM
SSD_GROUPS = 4
SSD_D_STATE = 128
SSD_CONV = 5
SSD_CHUNK = 128
SSD_XBC = SSD_D_INNER + 2 * SSD_GROUPS * SSD_D_STATE
CONF_D = 512
CONF_KERNEL = 31
POOL_D = 512
POOL_WINDOWS = (2, 4, 8, 16)
POOL_GROUPS = len(POOL_WINDOWS)
POOL_GROUP_D = POOL_D // POOL_GROUPS
N_BRANCH = 3
D_FF = 2816
N_MOD = 9
FFN_RES = 0.5
EPS = 1e-6
OFF_Z = SSD_D_INNER
OFF_XBC = OFF_Z + SSD_XBC
OFF_DT = OFF_XBC + 2 * SSD_HEADS
OFF_CONF = OFF_DT + 2 * CONF_D
OFF_POOL = OFF_CONF + POOL_D
IN_COLS = OFF_POOL + N_BRANCH * D_MODEL

kernel_name = 'hybrid_ssd_conformer_pool_diffusion_step'


def rmsnorm(x, g):
    x32 = x.astype(jnp.float32)
    r = x32 * lax.rsqrt(jnp.mean(x32 * x32, axis=-1, keepdims=True) + EPS)
    return (r * g.astype(jnp.float32)).astype(x.dtype)


def layernorm(x, g, b):
    x32 = x.astype(jnp.float32)
    mu = jnp.mean(x32, axis=-1, keepdims=True)
    var = jnp.mean(jnp.square(x32 - mu), axis=-1, keepdims=True)
    r = (x32 - mu) * lax.rsqrt(var + EPS)
    return (r * g.astype(jnp.float32) + b.astype(jnp.float32)).astype(x.dtype)


def dwconv(x, w, b):
    k = w.shape[0]
    y = lax.conv_general_dilated(x, w[:, None, :].astype(x.dtype), window_strides=(1,),
                                 padding=[(k // 2, k // 2)],
                                 dimension_numbers=('NWC', 'WIO', 'NWC'),
                                 feature_group_count=x.shape[-1])
    return y + b.astype(x.dtype)


def swiglu(h, w_in, w_out):
    g, u = jnp.split(h @ w_in, 2, axis=-1)
    return (jax.nn.silu(g) * u) @ w_out


def pos_embed_2d(n_tokens):
    rows = n_tokens // GRID_W
    r, col = jnp.meshgrid(jnp.arange(rows), jnp.arange(GRID_W), indexing='ij')
    r = r.reshape(-1).astype(jnp.float32)
    col = col.reshape(-1).astype(jnp.float32)
    q = D_MODEL // 4
    omega = 1.0 / (10000.0 ** (jnp.arange(q, dtype=jnp.float32) / q))
    ar = r[:, None] * omega
    ac = col[:, None] * omega
    return jnp.concatenate([jnp.sin(ar), jnp.cos(ar), jnp.sin(ac), jnp.cos(ac)], axis=-1)


def ssd_scan(x, dt, a_neg, B, C, h0):
    b, L, H, P = x.shape
    N = B.shape[-1]
    nc = L // SSD_CHUNK
    xr = (x * dt[..., None]).reshape(b, nc, SSD_CHUNK, H, P)
    Br = B.reshape(b, nc, SSD_CHUNK, H, N)
    Cr = C.reshape(b, nc, SSD_CHUNK, H, N)
    a_cs = jnp.cumsum((dt * a_neg).reshape(b, nc, SSD_CHUNK, H), axis=2)
    tri = jnp.tril(jnp.ones((SSD_CHUNK, SSD_CHUNK), dtype=bool))[None, None, :, :, None]
    seg = a_cs[:, :, :, None, :] - a_cs[:, :, None, :, :]
    decay_ls = jnp.exp(jnp.where(tri, seg, -jnp.inf))
    scores = jnp.einsum('bclhn,bcshn->bclsh', Cr, Br) * decay_ls
    y_diag = jnp.einsum('bclsh,bcshp->bclhp', scores, xr)
    decay_to_end = jnp.exp(a_cs[:, :, -1:, :] - a_cs)
    chunk_states = jnp.einsum('bclhn,bclhp->bchpn', Br * decay_to_end[..., None], xr)
    states = jnp.concatenate([h0[:, None], chunk_states], axis=1)
    cs_pad = jnp.concatenate([jnp.zeros((b, 1, H), a_cs.dtype),
                              jnp.cumsum(a_cs[:, :, -1, :], axis=1)], axis=1)
    tri_c = jnp.tril(jnp.ones((nc + 1, nc + 1), dtype=bool))[None, :, :, None]
    seg_c = cs_pad[:, :, None, :] - cs_pad[:, None, :, :]
    decay_c = jnp.exp(jnp.where(tri_c, seg_c, -jnp.inf))
    new_states = jnp.einsum('bzch,bchpn->bzhpn', decay_c, states)
    y_off = jnp.einsum('bclhn,bchpn->bclhp', Cr * jnp.exp(a_cs)[..., None], new_states[:, :-1])
    return (y_diag + y_off).reshape(b, L, H, P), new_states[:, -1]


def ssd_branch(z, xbc, dt_raw, h0, ssd_conv_w, ssd_conv_b, ssd_a_log, ssd_dt_bias, ssd_d,
               ssd_norm_g, w_br_ssd):
    b, L, _ = xbc.shape
    xbc = jax.nn.silu(dwconv(xbc, ssd_conv_w, ssd_conv_b))
    xs, Bm, Cm = jnp.split(xbc, [SSD_D_INNER, SSD_D_INNER + SSD_GROUPS * SSD_D_STATE], axis=-1)
    rep = SSD_HEADS // SSD_GROUPS
    xh = xs.reshape(b, L, SSD_HEADS, SSD_HEAD_DIM).astype(jnp.float32)
    Bh = jnp.repeat(Bm.reshape(b, L, SSD_GROUPS, SSD_D_STATE), rep, axis=2).astype(jnp.float32)
    Ch = jnp.repeat(Cm.reshape(b, L, SSD_GROUPS, SSD_D_STATE), rep, axis=2).astype(jnp.float32)
    dt = jax.nn.softplus(dt_raw.astype(jnp.float32).reshape(b, L, 2, SSD_HEADS)
                         + ssd_dt_bias.astype(jnp.float32))
    a_neg = -jnp.exp(ssd_a_log.astype(jnp.float32))
    h0 = h0.astype(jnp.float32)
    y_f, s_f = ssd_scan(xh, dt[:, :, 0], a_neg[0], Bh, Ch, h0[:, 0])
    y_b, s_b = ssd_scan(jnp.flip(xh, 1), jnp.flip(dt[:, :, 1], 1), a_neg[1],
                        jnp.flip(Bh, 1), jnp.flip(Ch, 1), h0[:, 1])
    y = y_f + jnp.flip(y_b, 1) + ssd_d.astype(jnp.float32)[:, None] * xh
    y = y.reshape(b, L, SSD_D_INNER) * jax.nn.silu(z.astype(jnp.float32))
    y = rmsnorm(y, ssd_norm_g).astype(z.dtype)
    return y @ w_br_ssd, jnp.stack([s_f, s_b], axis=1)


def conformer_branch(u, conf_conv_w, conf_conv_b, conf_ln_g, conf_ln_b, w_br_conf):
    a, g = jnp.split(u, 2, axis=-1)
    v = a * jax.nn.sigmoid(g)
    v = dwconv(v, conf_conv_w, conf_conv_b)
    v = jax.nn.silu(layernorm(v, conf_ln_g, conf_ln_b))
    return v @ w_br_conf


def pool_branch(u, pool_w, pool_scale, w_br_pool):
    b, L, _ = u.shape
    u32 = u.astype(jnp.float32)
    cs = jnp.concatenate([jnp.zeros((b, 1, POOL_D), jnp.float32), jnp.cumsum(u32, axis=1)], axis=1)
    t = jnp.arange(L)
    groups = []
    for gi, w in enumerate(POOL_WINDOWS):
        lo = jnp.clip(t - w // 2, 0, L)
        hi = jnp.clip(t - w // 2 + w, 0, L)
        sl = slice(gi * POOL_GROUP_D, (gi + 1) * POOL_GROUP_D)
        cs_g = cs[..., sl]
        mean = (jnp.take(cs_g, hi, axis=1) - jnp.take(cs_g, lo, axis=1)) \
            / (hi - lo).astype(jnp.float32)[None, :, None]
        groups.append(mean - u32[..., sl])
    pooled = jnp.stack(groups, axis=2)
    mixed = jnp.einsum('blgc,gcd->blgd', pooled, pool_w.astype(jnp.float32)).reshape(b, L, POOL_D)
    mixed = mixed * pool_scale.astype(jnp.float32)
    return mixed.astype(u.dtype) @ w_br_pool


def token_mixer(h, h0, w_in, ssd_conv_w, ssd_conv_b, ssd_a_log, ssd_dt_bias, ssd_d, ssd_norm_g,
                w_br_ssd, conf_conv_w, conf_conv_b, conf_ln_g, conf_ln_b, w_br_conf, pool_w,
                pool_scale, w_br_pool, w_out):
    proj = h @ w_in
    z, xbc, dt_raw, conf_in, pool_in, gate_in = jnp.split(
        proj, [OFF_Z, OFF_XBC, OFF_DT, OFF_CONF, OFF_POOL], axis=-1)
    br_ssd, st = ssd_branch(z, xbc, dt_raw, h0, ssd_conv_w, ssd_conv_b, ssd_a_log, ssd_dt_bias,
                            ssd_d, ssd_norm_g, w_br_ssd)
    br_conf = conformer_branch(conf_in, conf_conv_w, conf_conv_b, conf_ln_g, conf_ln_b, w_br_conf)
    br_pool = pool_branch(pool_in, pool_w, pool_scale, w_br_pool)
    g_ssd, g_conf, g_pool = jnp.split(jax.nn.sigmoid(gate_in), N_BRANCH, axis=-1)
    merged = g_ssd * br_ssd + g_conf * br_conf + g_pool * br_pool
    return merged @ w_out, st


def layer(x, cond, h0, w_mod, b_mod, norm_g, w_ffn_in, w_ffn_out, w_in, ssd_conv_w, ssd_conv_b,
          ssd_a_log, ssd_dt_bias, ssd_d, ssd_norm_g, w_br_ssd, conf_conv_w, conf_conv_b, conf_ln_g,
          conf_ln_b, w_br_conf, pool_w, pool_scale, w_br_pool, w_out):
    mod = jax.nn.silu(cond) @ w_mod + b_mod
    sh1, sc1, g1, sh2, sc2, g2, sh3, sc3, g3 = jnp.split(mod[:, None, :], N_MOD, axis=-1)
    h = rmsnorm(x, norm_g[0]) * (1 + sc1) + sh1
    x = x + FFN_RES * g1 * rmsnorm(swiglu(h, w_ffn_in[0], w_ffn_out[0]), norm_g[1])
    h = rmsnorm(x, norm_g[2]) * (1 + sc2) + sh2
    y, st = token_mixer(h, h0, w_in, ssd_conv_w, ssd_conv_b, ssd_a_log, ssd_dt_bias, ssd_d,
                        ssd_norm_g, w_br_ssd, conf_conv_w, conf_conv_b, conf_ln_g, conf_ln_b,
                        w_br_conf, pool_w, pool_scale, w_br_pool, w_out)
    x = x + g2 * rmsnorm(y, norm_g[3])
    h = rmsnorm(x, norm_g[4]) * (1 + sc3) + sh3
    x = x + FFN_RES * g3 * rmsnorm(swiglu(h, w_ffn_in[1], w_ffn_out[1]), norm_g[5])
    return x, st


def setup_inputs(seed: int = 0) -> dict:
    key = jax.random.key(seed)
    ks = jax.random.split(key, 32)
    f32 = jnp.float32

    def nrm(k, shape, fan_in):
        return jax.random.normal(k, shape, f32) * (fan_in ** -0.5)

    def gain(k, shape):
        return 1.0 + 0.05 * jax.random.normal(k, shape, f32)

    def small(k, shape):
        return 0.02 * jax.random.normal(k, shape, f32)

    dt0 = jnp.exp(jax.random.uniform(ks[10], (DEPTH, 2, SSD_HEADS), f32,
                                     float(np.log(1e-3)), float(np.log(1e-1))))
    return {
        'x_prompt': jax.random.normal(ks[0], (BATCH, SEQ, D_MODEL), f32),
        'x_sample': jax.random.normal(ks[1], (DEC_BATCH, DEC_SEQ, D_MODEL), f32),
        'state_ssd': 0.5 * jax.random.normal(
            ks[2], (DEC_BATCH, DEPTH, 2, SSD_HEADS, SSD_HEAD_DIM, SSD_D_STATE), f32),
        'c': jax.random.normal(ks[3], (DEC_BATCH, D_MODEL), f32),
        'c_ctx': jax.random.normal(ks[4], (D_MODEL,), f32),
        'w_mod': nrm(ks[5], (DEPTH, D_MODEL, N_MOD * D_MODEL), D_MODEL),
        'b_mod': small(ks[6], (DEPTH, N_MOD * D_MODEL)),
        'norm_g': gain(ks[7], (DEPTH, 6, D_MODEL)),
        'w_ffn_in': nrm(ks[8], (DEPTH, 2, D_MODEL, 2 * D_FF), D_MODEL),
        'w_ffn_out': nrm(ks[9], (DEPTH, 2, D_FF, D_MODEL), D_FF),
        'w_in': nrm(ks[11], (DEPTH, D_MODEL, IN_COLS), D_MODEL),
        'ssd_conv_w': nrm(ks[12], (DEPTH, SSD_CONV, SSD_XBC), SSD_CONV),
        'ssd_conv_b': small(ks[13], (DEPTH, SSD_XBC)),
        'ssd_a_log': jnp.log(jax.random.uniform(ks[14], (DEPTH, 2, SSD_HEADS), f32, 1.0, 16.0)),
        'ssd_dt_bias': dt0 + jnp.log(-jnp.expm1(-dt0)),
        'ssd_d': 1.0 + 0.1 * jax.random.normal(ks[15], (DEPTH, SSD_HEADS), f32),
        'ssd_norm_g': gain(ks[16], (DEPTH, SSD_D_INNER)),
        'w_br_ssd': nrm(ks[17], (DEPTH, SSD_D_INNER, D_MODEL), SSD_D_INNER),
        'conf_conv_w': nrm(ks[18], (DEPTH, CONF_KERNEL, CONF_D), CONF_KERNEL),
        'conf_conv_b': small(ks[19], (DEPTH, CONF_D)),
        'conf_ln_g': gain(ks[20], (DEPTH, CONF_D)),
        'conf_ln_b': small(ks[21], (DEPTH, CONF_D)),
        'w_br_conf': nrm(ks[22], (DEPTH, CONF_D, D_MODEL), CONF_D),
        'pool_w': nrm(ks[23], (DEPTH, POOL_GROUPS, POOL_GROUP_D, POOL_GROUP_D), POOL_GROUP_D),
        'pool_scale': 1.0 + 0.1 * jax.random.normal(ks[24], (DEPTH, POOL_D), f32),
        'w_br_pool': nrm(ks[25], (DEPTH, POOL_D, D_MODEL), POOL_D),
        'w_out': nrm(ks[26], (DEPTH, D_MODEL, D_MODEL), D_MODEL),
    }


def reference(x_prompt, x_sample, state_ssd, c, c_ctx, w_mod, b_mod, norm_g, w_ffn_in, w_ffn_out,
              w_in, ssd_conv_w, ssd_conv_b, ssd_a_log, ssd_dt_bias, ssd_d, ssd_norm_g, w_br_ssd,
              conf_conv_w, conf_conv_b, conf_ln_g, conf_ln_b, w_br_conf, pool_w, pool_scale,
              w_br_pool, w_out):
    ctx_cond = c_ctx[None, :]
    xp = x_prompt
    zero_state = jnp.zeros((x_prompt.shape[0], 2, SSD_HEADS, SSD_HEAD_DIM, SSD_D_STATE), jnp.float32)
    xs = x_sample + pos_embed_2d(x_sample.shape[1]).astype(x_sample.dtype)[None]
    ctx_states = []
    for l in range(DEPTH):
        lp = (w_mod[l], b_mod[l], norm_g[l], w_ffn_in[l], w_ffn_out[l], w_in[l], ssd_conv_w[l],
              ssd_conv_b[l], ssd_a_log[l], ssd_dt_bias[l], ssd_d[l], ssd_norm_g[l], w_br_ssd[l],
              conf_conv_w[l], conf_conv_b[l], conf_ln_g[l], conf_ln_b[l], w_br_conf[l], pool_w[l],
              pool_scale[l], w_br_pool[l], w_out[l])
        xp, st = layer(xp, ctx_cond, zero_state, *lp)
        ctx_states.append(st)
        xs, _ = layer(xs, c, state_ssd[:, l], *lp)
    new_state_ssd = jnp.stack(ctx_states, axis=1).astype(x_prompt.dtype)
    return (xp, xs, new_state_ssd)
```

```python
import functools

import jax
import jax.numpy as jnp
from jax import lax
from jax.experimental import pallas as pl
from jax.experimental.pallas import tpu as pltpu

F32 = jnp.float32
BF16 = jnp.bfloat16

D_MODEL = 1024
BATCH = 32
SEQ = 256
DEPTH = 4
DEC_BATCH = 2
DEC_SEQ = 1024
GRID_W = 64
SSD_D_INNER = 1024
SSD_HEAD_DIM = 64
SSD_HEADS = 16
SSD_GROUPS = 4
SSD_D_STATE = 128
SSD_CONV = 5
SSD_CHUNK = 128
SSD_XBC = 2048
CONF_D = 512
CONF_KERNEL = 31
POOL_D = 512
POOL_WINDOWS = (2, 4, 8, 16)
D_FF = 2816
N_MOD = 9
FFN_RES = 0.5
EPS = 1e-6
OFF_Z = 1024
OFF_XBC = 3072
OFF_DT = 3104
OFF_CONF = 4128
OFF_POOL = 4640
IN_COLS = 7712

T_CTX = BATCH * SEQ
T_ALL = T_CTX + DEC_BATCH * DEC_SEQ
TILE = 256
HALO = 16
N_TILES = T_ALL // TILE
CTX_TILES = T_CTX // TILE
LAT_TILES = DEC_SEQ // TILE
N_CHUNKS = T_ALL // SSD_CHUNK
CTX_CHUNKS = T_CTX // SSD_CHUNK
SEQ_CHUNKS = SEQ // SSD_CHUNK
LAT_CHUNKS = DEC_SEQ // SSD_CHUNK
N_PAIRS = SSD_HEADS // 2

FF_CHUNK = 256
N_FF_CHUNKS = D_FF // FF_CHUNK
TM_FFN = 512

PC_Z = 0
PC_XBC = 1024
PC_CONF = 3072
PC_POOL = 4096
PC_GATE = 4608
PC_DT = 7680
PC_END = 7936
DT_LANES = 128

VMEM_LIMIT = 56 * 1024 * 1024


def _cparams(sem):
    return pltpu.CompilerParams(dimension_semantics=sem, vmem_limit_bytes=VMEM_LIMIT)


def _resident(shape):
    nd = len(shape)
    return pl.BlockSpec(shape, lambda *_: (0,) * nd, pipeline_mode=pl.Buffered(1))


def _rms(x, g):
    ms = jnp.mean(x * x, axis=-1, keepdims=True)
    return x * lax.rsqrt(ms + EPS) * g


def _silu(x):
    return x * jax.nn.sigmoid(x)


def _softplus(x):
    return jnp.maximum(x, 0.0) + jnp.log(1.0 + jnp.exp(-jnp.abs(x)))


def _mod_row(i, tm):
    ctx_tiles = T_CTX // tm
    per_seq = DEC_SEQ // tm
    return jnp.where(i < ctx_tiles, 0, 1 + jnp.maximum(i - ctx_tiles, 0) // per_seq)


def _mod_vec(mod_ref, row, k):
    return mod_ref[pl.ds(row, 1), pl.ds(k * D_MODEL, D_MODEL)]


def _mod_kernel(ct_ref, w_ref, b_ref, o_ref):
    ct = ct_ref[...]
    s = _silu(ct)
    w = w_ref[0]
    b = b_ref[0]
    o_ref[0] = jnp.zeros(o_ref.shape[1:], F32)
    for r in range(1 + DEC_BATCH):
        o_ref[0, r:r + 1, :] = jnp.sum(s[:, r:r + 1] * w, axis=0, keepdims=True) + b


def _modulation(cond_t, w_mod, b_mod):
    tn = 1024
    n_cols = N_MOD * D_MODEL
    return pl.pallas_call(
        _mod_kernel,
        grid=(DEPTH, n_cols // tn),
        in_specs=[
            pl.BlockSpec((D_MODEL, 8), lambda l, j: (0, 0)),
            pl.BlockSpec((1, D_MODEL, tn), lambda l, j: (l, 0, j)),
            pl.BlockSpec((1, 1, tn), lambda l, j: (l, 0, j)),
        ],
        out_specs=pl.BlockSpec((1, 8, tn), lambda l, j: (l, 0, j)),
        out_shape=jax.ShapeDtypeStruct((DEPTH, 8, n_cols), F32),
        compiler_params=_cparams(("arbitrary", "arbitrary")),
        name="modulation",
    )(cond_t, w_mod, b_mod.reshape(DEPTH, 1, n_cols))


def _ffn_kernel(x_ref, mod_ref, gpre_ref, gpost_ref, wgu_ref, wo_ref, o_ref, acc_ref, *, tm, k0):
    row = _mod_row(pl.program_id(0), tm)
    x = x_ref[...]
    sh = _mod_vec(mod_ref, row, k0)
    sc = _mod_vec(mod_ref, row, k0 + 1)
    gt = _mod_vec(mod_ref, row, k0 + 2)
    h = (_rms(x, gpre_ref[...]) * (1.0 + sc) + sh).astype(BF16)
    for c in range(N_FF_CHUNKS):
        gu = jnp.dot(h, wgu_ref[c], preferred_element_type=F32)
        a = (_silu(gu[:, :FF_CHUNK]) * gu[:, FF_CHUNK:]).astype(BF16)
        part = jnp.dot(a, wo_ref[c], preferred_element_type=F32)
        if c == 0:
            acc_ref[...] = part
        else:
            acc_ref[...] += part
    o_ref[...] = x + (FFN_RES * gt) * _rms(acc_ref[...], gpost_ref[...])


def _ffn(x, mod_l, g_pre, g_post, wgu, wo, k0):
    tm = TM_FFN
    return pl.pallas_call(
        functools.partial(_ffn_kernel, tm=tm, k0=k0),
        grid=(T_ALL // tm,),
        in_specs=[
            pl.BlockSpec((tm, D_MODEL), lambda i: (i, 0)),
            _resident((8, N_MOD * D_MODEL)),
            _resident((1, D_MODEL)),
            _resident((1, D_MODEL)),
            _resident((N_FF_CHUNKS, D_MODEL, 2 * FF_CHUNK)),
            _resident((N_FF_CHUNKS, FF_CHUNK, D_MODEL)),
        ],
        out_specs=pl.BlockSpec((tm, D_MODEL), lambda i: (i, 0)),
        out_shape=jax.ShapeDtypeStruct((T_ALL, D_MODEL), F32),
        scratch_shapes=[pltpu.VMEM((tm, D_MODEL), F32)],
        compiler_params=_cparams(("arbitrary",)),
        name="ffn",
    )(x, mod_l, g_pre, g_post, wgu, wo)


def _proj_kernel(x_ref, mod_ref, g_ref, w_ref, wdt_t_ref, dtb_row_ref, dtb_col_ref,
                 z_ref, xbc_ref, v_ref, pool_ref, gate_ref, dt_ref, dt_t_ref):
    row = _mod_row(pl.program_id(0), TILE)
    sh = _mod_vec(mod_ref, row, 3)
    sc = _mod_vec(mod_ref, row, 4)
    h = (_rms(x_ref[...], g_ref[...]) * (1.0 + sc) + sh).astype(BF16)

    def mm(lo, hi):
        return jnp.dot(h, w_ref[:, lo:hi], preferred_element_type=F32)

    z_ref[...] = _silu(mm(PC_Z, PC_XBC))
    xbc_ref[...] = mm(PC_XBC, PC_CONF)
    ag = mm(PC_CONF, PC_POOL)
    v_ref[...] = ag[:, :CONF_D] * jax.nn.sigmoid(ag[:, CONF_D:])
    pool_ref[...] = mm(PC_POOL, PC_GATE)
    gate_ref[...] = jax.nn.sigmoid(mm(PC_GATE, PC_DT))
    dt_ref[...] = _softplus(mm(PC_DT, PC_END) + dtb_row_ref[...])
    dt_t = lax.dot_general(wdt_t_ref[...], h, (((1,), (1,)), ((), ())), preferred_element_type=F32)
    dt_t_ref[...] = _softplus(dt_t + dtb_col_ref[...])


def _proj(x, mod_l, g, w, wdt_t, dtb_row, dtb_col):
    def tile(n):
        return pl.BlockSpec((TILE, n), lambda i: (i, 0))

    widths = (SSD_D_INNER, SSD_XBC, CONF_D, POOL_D, 3 * D_MODEL, 2 * DT_LANES)
    return pl.pallas_call(
        _proj_kernel,
        grid=(N_TILES,),
        in_specs=[
            tile(D_MODEL),
            _resident((8, N_MOD * D_MODEL)),
            _resident((1, D_MODEL)),
            _resident((D_MODEL, PC_END)),
            _resident((2 * SSD_HEADS, D_MODEL)),
            _resident((1, 2 * DT_LANES)),
            _resident((2 * SSD_HEADS, 1)),
        ],
        out_specs=[tile(n) for n in widths] + [pl.BlockSpec((2 * SSD_HEADS, TILE), lambda i: (0, i))],
        out_shape=[jax.ShapeDtypeStruct((T_ALL, n), F32) for n in widths]
        + [jax.ShapeDtypeStruct((2 * SSD_HEADS, T_ALL), F32)],
        compiler_params=_cparams(("arbitrary",)),
        name="in_proj",
    )(x, mod_l, g, w, wdt_t, dtb_row, dtb_col)


def _fill_padded(pad_ref, cur_ref, prev_ref, next_ref, has_prev, has_next):
    pad_ref[0:HALO, :] = jnp.where(has_prev, prev_ref[...], 0.0)
    pad_ref[HALO:HALO + TILE, :] = cur_ref[...]
    pad_ref[HALO + TILE:, :] = jnp.where(has_next, next_ref[...], 0.0)


def _dwconv_block(pad_ref, w_ref, col, n_taps):
    first = HALO - n_taps // 2
    acc = None
    for k in range(n_taps):
        term = w_ref[pl.ds(k, 1), pl.ds(col, 128)] * pad_ref[pl.ds(first + k, TILE), pl.ds(col, 128)]
        acc = term if acc is None else acc + term
    return acc


def _local_kernel(xc_ref, xp_ref, xn_ref, vc_ref, vp_ref, vn_ref, pc_ref, pp_ref, pn_ref,
                  scw_ref, scb_ref, ccw_ref, ccb_ref, lng_ref, lnb_ref, wbc_ref,
                  pw_ref, psc_ref, wbp_ref,
                  xo_ref, bc_ref, bp_ref,
                  xpad_ref, vpad_ref, ppad_ref, cv_ref, mix_ref):
    i = pl.program_id(0)
    lat = i >= CTX_TILES
    k = jnp.maximum(i - CTX_TILES, 0) % LAT_TILES
    has_prev = jnp.logical_and(lat, k != 0)
    has_next = jnp.logical_and(lat, k != LAT_TILES - 1)
    _fill_padded(xpad_ref, xc_ref, xp_ref, xn_ref, has_prev, has_next)
    _fill_padded(vpad_ref, vc_ref, vp_ref, vn_ref, has_prev, has_next)
    _fill_padded(ppad_ref, pc_ref, pp_ref, pn_ref, has_prev, has_next)

    def ssd_cols(cb, carry):
        col = pl.multiple_of(cb * 128, 128)
        y = _dwconv_block(xpad_ref, scw_ref, col, SSD_CONV) + scb_ref[:, pl.ds(col, 128)]
        xo_ref[:, pl.ds(col, 128)] = _silu(y)
        return carry

    lax.fori_loop(0, SSD_XBC // 128, ssd_cols, 0)

    def conf_cols(cb, carry):
        col = pl.multiple_of(cb * 128, 128)
        cv_ref[:, pl.ds(col, 128)] = (_dwconv_block(vpad_ref, ccw_ref, col, CONF_KERNEL)
                                      + ccb_ref[:, pl.ds(col, 128)])
        return carry

    lax.fori_loop(0, CONF_D // 128, conf_cols, 0)
    cv = cv_ref[...]
    mu = jnp.mean(cv, axis=-1, keepdims=True)
    cen = cv - mu
    var = jnp.mean(cen * cen, axis=-1, keepdims=True)
    ln = cen * lax.rsqrt(var + EPS) * lng_ref[...] + lnb_ref[...]
    bc_ref[...] = jnp.dot(_silu(ln).astype(BF16), wbc_ref[...], preferred_element_type=F32)

    seq_len = jnp.where(lat, DEC_SEQ, SEQ)
    pos = k * TILE + lax.broadcasted_iota(jnp.int32, (TILE, 1), 0)
    for gi, w in enumerate(POOL_WINDOWS):
        cols = slice(gi * 128, (gi + 1) * 128)
        first = HALO - w // 2
        s = ppad_ref[first:first + TILE, cols]
        for j in range(1, w):
            s = s + ppad_ref[first + j:first + j + TILE, cols]
        lo = jnp.maximum(pos - w // 2, 0)
        hi = jnp.minimum(pos - w // 2 + w, seq_len)
        pooled = s / (hi - lo).astype(F32) - pc_ref[:, cols]
        mixed = jnp.dot(pooled.astype(BF16), pw_ref[gi], preferred_element_type=F32)
        mix_ref[:, cols] = mixed * psc_ref[:, cols]
    bp_ref[...] = jnp.dot(mix_ref[...].astype(BF16), wbp_ref[...], preferred_element_type=F32)


def _local(xbc, v, pool, scw, scb, ccw, ccb, lng, lnb, wbc, pw, psc, wbp):
    per_tile = TILE // HALO
    n_halo = T_ALL // HALO

    def cur(n):
        return pl.BlockSpec((TILE, n), lambda i: (i, 0))

    def prev(n):
        return pl.BlockSpec((HALO, n), lambda i: (jnp.maximum(i * per_tile - 1, 0), 0))

    def nxt(n):
        return pl.BlockSpec((HALO, n), lambda i: (jnp.minimum((i + 1) * per_tile, n_halo - 1), 0))

    def trio(n):
        return [cur(n), prev(n), nxt(n)]

    return pl.pallas_call(
        _local_kernel,
        grid=(N_TILES,),
        in_specs=trio(SSD_XBC) + trio(CONF_D) + trio(POOL_D) + [
            _resident((SSD_CONV, SSD_XBC)), _resident((1, SSD_XBC)),
            _resident((CONF_KERNEL, CONF_D)), _resident((1, CONF_D)),
            _resident((1, CONF_D)), _resident((1, CONF_D)),
            _resident((CONF_D, D_MODEL)),
            _resident((len(POOL_WINDOWS), 128, 128)), _resident((1, POOL_D)),
            _resident((POOL_D, D_MODEL)),
        ],
        out_specs=[cur(SSD_XBC), cur(D_MODEL), cur(D_MODEL)],
        out_shape=[jax.ShapeDtypeStruct((T_ALL, SSD_XBC), F32),
                   jax.ShapeDtypeStruct((T_ALL, D_MODEL), F32),
                   jax.ShapeDtypeStruct((T_ALL, D_MODEL), F32)],
        scratch_shapes=[pltpu.VMEM((TILE + 2 * HALO, SSD_XBC), F32),
                        pltpu.VMEM((TILE + 2 * HALO, CONF_D), F32),
                        pltpu.VMEM((TILE + 2 * HALO, POOL_D), F32),
                        pltpu.VMEM((TILE, CONF_D), F32),
                        pltpu.VMEM((TILE, POOL_D), F32)],
        compiler_params=_cparams(("arbitrary",)),
        name="local_branches",
    )(xbc, xbc, xbc, v, v, v, pool, pool, pool, scw, scb, ccw, ccb, lng, lnb, wbc, pw, psc, wbp)


def _scan_chunk(d, j):
    return jnp.where(d == 0, j, N_CHUNKS - 1 - j)


def _scan_kernel(xc_ref, dt_ref, dt_t_ref, tri_ref, alr_ref, alc_ref, dsk_ref, h0_ref,
                 y_ref, ns_ref, s_ref):
    d = pl.program_id(0)
    c = _scan_chunk(d, pl.program_id(1))
    is_ctx = c < CTX_CHUNKS
    ci = jnp.where(is_ctx, c % SEQ_CHUNKS, jnp.maximum(c - CTX_CHUNKS, 0) % LAT_CHUNKS)
    n_seq = jnp.where(is_ctx, SEQ_CHUNKS, LAT_CHUNKS)
    first = ci == d * (n_seq - 1)
    last = ci == (1 - d) * (n_seq - 1)

    @pl.when(jnp.logical_and(first, is_ctx))
    def _():
        s_ref[...] = jnp.zeros(s_ref.shape, F32)

    @pl.when(jnp.logical_and(first, jnp.logical_not(is_ctx)))
    def _():
        s_ref[...] = h0_ref[0, 0]

    tri = tri_ref[0]
    mask = tri > 0.5
    a_row = -jnp.exp(alr_ref[0])
    a_col = -jnp.exp(alc_ref[0])
    dt = dt_ref[:, :SSD_HEADS]
    hp = lax.Precision.HIGHEST
    acs = jnp.dot(tri, dt, precision=hp, preferred_element_type=F32) * a_row
    acs_t = lax.dot_general(dt_t_ref[...], tri, (((1,), (1,)), ((), ())), precision=hp,
                            preferred_element_type=F32) * a_col
    total = jnp.sum(dt, axis=0, keepdims=True) * a_row
    e_acs = jnp.exp(acs)
    d_end = jnp.exp(total - acs)
    e_tot = jnp.exp(total)
    lane_lo = lax.broadcasted_iota(jnp.int32, (SSD_CHUNK, 128), 1) < SSD_HEAD_DIM
    row_lo = lax.broadcasted_iota(jnp.int32, (128, SSD_D_STATE), 0) < SSD_HEAD_DIM
    skip_on = (d == 0).astype(F32)
    nt = (((1,), (1,)), ((), ()))
    tn = (((0,), (0,)), ((), ()))

    for g in range(SSD_GROUPS):
        b_off = SSD_D_INNER + g * SSD_D_STATE
        c_off = SSD_D_INNER + SSD_GROUPS * SSD_D_STATE + g * SSD_D_STATE
        bg = xc_ref[:, b_off:b_off + SSD_D_STATE]
        cg = xc_ref[:, c_off:c_off + SSD_D_STATE]
        cb = lax.dot_general(cg.astype(BF16), bg.astype(BF16), nt, preferred_element_type=F32)
        for q in range(2 * g, 2 * g + 2):
            xp = xc_ref[:, q * 128:(q + 1) * 128]
            s_old = s_ref[q]
            h0, h1 = 2 * q, 2 * q + 1
            xr = xp * jnp.where(lane_lo, dt[:, h0:h0 + 1], dt[:, h1:h1 + 1])
            y_pair = jnp.zeros((SSD_CHUNK, 128), F32)
            cs = jnp.zeros((128, SSD_D_STATE), F32)
            for h, keep_lane, keep_row in ((h0, lane_lo, row_lo),
                                           (h1, jnp.logical_not(lane_lo), jnp.logical_not(row_lo))):
                xr_h = jnp.where(keep_lane, xr, 0.0).astype(BF16)
                seg = jnp.where(mask, acs[:, h:h + 1] - acs_t[h:h + 1, :], -jnp.inf)
                m = (cb * jnp.exp(seg)).astype(BF16)
                ce = (cg * e_acs[:, h:h + 1]).astype(BF16)
                bd = (bg * d_end[:, h:h + 1]).astype(BF16)
                s_h = jnp.where(keep_row, s_old, 0.0).astype(BF16)
                y_pair += jnp.dot(m, xr_h, preferred_element_type=F32)
                y_pair += lax.dot_general(ce, s_h, nt, preferred_element_type=F32)
                cs += lax.dot_general(xr_h, bd, tn, preferred_element_type=F32)
            decay = jnp.where(row_lo[:, :1], e_tot[:, h0:h0 + 1], e_tot[:, h1:h1 + 1])
            s_ref[q] = s_old * decay + cs
            y_ref[0, :, q * 128:(q + 1) * 128] = y_pair + (skip_on * dsk_ref[:, q * 128:(q + 1) * 128]) * xp

    @pl.when(jnp.logical_and(last, is_ctx))
    def _():
        ns_ref[0, 0] = s_ref[...]


def _scan(xconv, dt, dt_t, tri, alr, alc, dskip, h0):
    def chunk_of(d, j):
        return _scan_chunk(d, j)

    def lat_seq(d, j):
        return jnp.clip((chunk_of(d, j) - CTX_CHUNKS) // LAT_CHUNKS, 0, DEC_BATCH - 1)

    def ctx_seq(d, j):
        return jnp.minimum(chunk_of(d, j) // SEQ_CHUNKS, BATCH - 1)

    state_block = (1, 1, N_PAIRS, 128, SSD_D_STATE)
    return pl.pallas_call(
        _scan_kernel,
        grid=(2, N_CHUNKS),
        in_specs=[
            pl.BlockSpec((SSD_CHUNK, SSD_XBC), lambda d, j: (chunk_of(d, j), 0)),
            pl.BlockSpec((SSD_CHUNK, DT_LANES), lambda d, j: (chunk_of(d, j), d)),
            pl.BlockSpec((SSD_HEADS, SSD_CHUNK), lambda d, j: (d, chunk_of(d, j))),
            pl.BlockSpec((1, SSD_CHUNK, SSD_CHUNK), lambda d, j: (d, 0, 0)),
            pl.BlockSpec((1, 1, SSD_HEADS), lambda d, j: (d, 0, 0)),
            pl.BlockSpec((1, SSD_HEADS, 1), lambda d, j: (d, 0, 0)),
            pl.BlockSpec((1, SSD_D_INNER), lambda d, j: (0, 0)),
            pl.BlockSpec(state_block, lambda d, j: (lat_seq(d, j), d, 0, 0, 0)),
        ],
        out_specs=[
            pl.BlockSpec((1, SSD_CHUNK, SSD_D_INNER), lambda d, j: (d, chunk_of(d, j), 0)),
            pl.BlockSpec(state_block, lambda d, j: (ctx_seq(d, j), d, 0, 0, 0)),
        ],
        out_shape=[jax.ShapeDtypeStruct((2, T_ALL, SSD_D_INNER), F32),
                   jax.ShapeDtypeStruct((BATCH, 2, N_PAIRS, 128, SSD_D_STATE), F32)],
        scratch_shapes=[pltpu.VMEM((N_PAIRS, 128, SSD_D_STATE), F32)],
        compiler_params=_cparams(("arbitrary", "arbitrary")),
        name="ssd_scan",
    )(xconv, dt, dt_t, tri, alr, alc, dskip, h0)


def _merge_kernel(x_ref, mod_ref, y_ref, z_ref, gate_ref, bc_ref, bp_ref, sng_ref, wbs_ref, wout_ref,
                  gpost_ref, o_ref):
    row = _mod_row(pl.program_id(0), TILE)
    gt = _mod_vec(mod_ref, row, 5)
    ys = (y_ref[0] + y_ref[1]) * z_ref[...]
    br_ssd = jnp.dot(_rms(ys, sng_ref[...]).astype(BF16), wbs_ref[...], preferred_element_type=F32)
    merged = (gate_ref[:, 0:D_MODEL] * br_ssd
              + gate_ref[:, D_MODEL:2 * D_MODEL] * bc_ref[...]
              + gate_ref[:, 2 * D_MODEL:] * bp_ref[...])
    yo = jnp.dot(merged.astype(BF16), wout_ref[...], preferred_element_type=F32)
    o_ref[...] = x_ref[...] + gt * _rms(yo, gpost_ref[...])


def _merge(x, mod_l, y, z, gate, bc, bp, sng, wbs, wout, g_post):
    def tile(n):
        return pl.BlockSpec((TILE, n), lambda i: (i, 0))

    return pl.pallas_call(
        _merge_kernel,
        grid=(N_TILES,),
        in_specs=[
            tile(D_MODEL),
            _resident((8, N_MOD * D_MODEL)),
            pl.BlockSpec((2, TILE, SSD_D_INNER), lambda i: (0, i, 0)),
            tile(SSD_D_INNER), tile(3 * D_MODEL), tile(D_MODEL), tile(D_MODEL),
            _resident((1, SSD_D_INNER)),
            _resident((SSD_D_INNER, D_MODEL)),
            _resident((D_MODEL, D_MODEL)),
            _resident((1, D_MODEL)),
        ],
        out_specs=tile(D_MODEL),
        out_shape=jax.ShapeDtypeStruct((T_ALL, D_MODEL), F32),
        compiler_params=_cparams(("arbitrary",)),
        name="merge_out",
    )(x, mod_l, y, z, gate, bc, bp, sng, wbs, wout, g_post)


def _pos_embed_2d(n_tokens):
    rows = n_tokens // GRID_W
    r, col = jnp.meshgrid(jnp.arange(rows), jnp.arange(GRID_W), indexing='ij')
    r = r.reshape(-1).astype(F32)
    col = col.reshape(-1).astype(F32)
    q = D_MODEL // 4
    omega = 1.0 / (10000.0 ** (jnp.arange(q, dtype=F32) / q))
    ar = r[:, None] * omega
    ac = col[:, None] * omega
    return jnp.concatenate([jnp.sin(ar), jnp.cos(ar), jnp.sin(ac), jnp.cos(ac)], axis=-1)


def _ffn_weights(w_in, w_out):
    g = w_in[:, :D_FF].reshape(D_MODEL, N_FF_CHUNKS, FF_CHUNK)
    u = w_in[:, D_FF:].reshape(D_MODEL, N_FF_CHUNKS, FF_CHUNK)
    wgu = jnp.concatenate([g, u], axis=-1).transpose(1, 0, 2).astype(BF16)
    wo = w_out.reshape(N_FF_CHUNKS, FF_CHUNK, D_MODEL).astype(BF16)
    return wgu, wo


def _proj_weights(w_in, dt_bias):
    pad = jnp.zeros((D_MODEL, DT_LANES - SSD_HEADS), F32)
    w_dt = w_in[:, OFF_XBC:OFF_DT]
    w = jnp.concatenate([
        w_in[:, :OFF_XBC], w_in[:, OFF_DT:],
        w_dt[:, :SSD_HEADS], pad, w_dt[:, SSD_HEADS:], pad], axis=1).astype(BF16)
    zpad = jnp.zeros((DT_LANES - SSD_HEADS,), F32)
    dtb_row = jnp.concatenate([dt_bias[0], zpad, dt_bias[1], zpad]).reshape(1, 2 * DT_LANES)
    dtb_col = dt_bias.reshape(2 * SSD_HEADS, 1)
    return w, w_dt.T.astype(BF16), dtb_row, dtb_col


def kernel(x_prompt, x_sample, state_ssd, c, c_ctx, w_mod, b_mod, norm_g, w_ffn_in, w_ffn_out, w_in,
           ssd_conv_w, ssd_conv_b, ssd_a_log, ssd_dt_bias, ssd_d, ssd_norm_g, w_br_ssd, conf_conv_w,
           conf_conv_b, conf_ln_g, conf_ln_b, w_br_conf, pool_w, pool_scale, w_br_pool, w_out):
    xs = x_sample + _pos_embed_2d(DEC_SEQ).astype(x_sample.dtype)[None]
    x = jnp.concatenate([x_prompt.reshape(T_CTX, D_MODEL), xs.reshape(DEC_BATCH * DEC_SEQ, D_MODEL)], axis=0)

    cond_t = jnp.concatenate([c_ctx[None, :], c, jnp.zeros((8 - 1 - DEC_BATCH, D_MODEL), F32)], axis=0).T
    mod = _modulation(cond_t, w_mod, b_mod)

    idx = jnp.arange(SSD_CHUNK)
    tri = jnp.stack([idx[:, None] >= idx[None, :], idx[:, None] <= idx[None, :]]).astype(F32)

    states = []
    for l in range(DEPTH):
        ng = norm_g[l].reshape(6, 1, D_MODEL)
        wgu0, wo0 = _ffn_weights(w_ffn_in[l, 0], w_ffn_out[l, 0])
        wgu1, wo1 = _ffn_weights(w_ffn_in[l, 1], w_ffn_out[l, 1])
        wp, wdt_t, dtb_row, dtb_col = _proj_weights(w_in[l], ssd_dt_bias[l])

        x = _ffn(x, mod[l], ng[0], ng[1], wgu0, wo0, 0)
        z, xbc, v, pool, gate, dt, dt_t = _proj(x, mod[l], ng[2], wp, wdt_t, dtb_row, dtb_col)
        xconv, br_conf, br_pool = _local(
            xbc, v, pool,
            ssd_conv_w[l], ssd_conv_b[l].reshape(1, SSD_XBC),
            conf_conv_w[l], conf_conv_b[l].reshape(1, CONF_D),
            conf_ln_g[l].reshape(1, CONF_D), conf_ln_b[l].reshape(1, CONF_D),
            w_br_conf[l].astype(BF16), pool_w[l].astype(BF16), pool_scale[l].reshape(1, POOL_D),
            w_br_pool[l].astype(BF16))
        h0 = state_ssd[:, l].reshape(DEC_BATCH, 2, N_PAIRS, 128, SSD_D_STATE)
        y, st = _scan(xconv, dt, dt_t, tri,
                      ssd_a_log[l].reshape(2, 1, SSD_HEADS), ssd_a_log[l].reshape(2, SSD_HEADS, 1),
                      jnp.repeat(ssd_d[l], SSD_HEAD_DIM).reshape(1, SSD_D_INNER), h0)
        states.append(st.reshape(BATCH, 2, SSD_HEADS, SSD_HEAD_DIM, SSD_D_STATE))
        x = _merge(x, mod[l], y, z, gate, br_conf, br_pool,
                   ssd_norm_g[l].reshape(1, SSD_D_INNER), w_br_ssd[l].astype(BF16), w_out[l].astype(BF16), ng[3])
        x = _ffn(x, mod[l], ng[4], ng[5], wgu1, wo1, 6)

    y_prompt = x[:T_CTX].reshape(BATCH, SEQ, D_MODEL)
    y_sample = x[T_CTX:].reshape(DEC_BATCH, DEC_SEQ, D_MODEL)
    new_state = jnp.stack(states, axis=1).astype(x_prompt.dtype)
    return (y_prompt, y_sample, new_state)
```

```python
import functools

import jax
import jax.numpy as jnp
from jax import lax
from jax.experimental import pallas as pl
from jax.experimental.pallas import tpu as pltpu

F32 = jnp.float32
BF16 = jnp.bfloat16

D_MODEL = 1024
BATCH = 32
SEQ = 256
DEPTH = 4
DEC_BATCH = 2
DEC_SEQ = 1024
GRID_W = 64
SSD_D_INNER = 1024
SSD_HEAD_DIM = 64
SSD_HEADS = 16
SSD_GROUPS = 4
SSD_D_STATE = 128
SSD_CONV = 5
SSD_CHUNK = 128
SSD_XBC = 2048
CONF_D = 512
CONF_KERNEL = 31
POOL_D = 512
POOL_WINDOWS = (2, 4, 8, 16)
D_FF = 2816
N_MOD = 9
FFN_RES = 0.5
EPS = 1e-6
OFF_XBC = 3072
OFF_DT = 3104
IN_COLS = 7712

T_CTX = BATCH * SEQ
T_LAT = DEC_BATCH * DEC_SEQ
T_ALL = T_CTX + T_LAT
TILE = 256
HALO = 16
N_TILES = T_ALL // TILE
CTX_TILES = T_CTX // TILE
LAT_TILES = DEC_SEQ // TILE
HEADS_PER_GROUP = SSD_HEADS // SSD_GROUPS
GROUP_ROWS = HEADS_PER_GROUP * SSD_HEAD_DIM

FF_CHUNK = 256
N_FF_CHUNKS = D_FF // FF_CHUNK
TM_FFN = 512

N_WA = OFF_XBC
N_WB = IN_COLS - OFF_DT
PB_POOL = 2 * CONF_D
PB_GATE = PB_POOL + POOL_D
DT_LANES = 128

VMEM_LIMIT = 56 * 1024 * 1024


def _cparams(sem):
    return pltpu.CompilerParams(dimension_semantics=sem, vmem_limit_bytes=VMEM_LIMIT)


def _layer_block(l, shape):
    nd = len(shape)
    return pl.BlockSpec((1,) + tuple(shape), lambda *_: (l,) + (0,) * nd, pipeline_mode=pl.Buffered(1))


def _rms(x, g):
    ms = jnp.mean(x * x, axis=-1, keepdims=True)
    return x * lax.rsqrt(ms + EPS) * g


def _silu(x):
    return x * jax.nn.sigmoid(x)


def _softplus(x):
    return jnp.maximum(x, 0.0) + jnp.log(1.0 + jnp.exp(-jnp.abs(x)))


def _mod_row(i, tm):
    ctx_tiles = T_CTX // tm
    per_seq = DEC_SEQ // tm
    return jnp.where(i < ctx_tiles, 0, 1 + jnp.maximum(i - ctx_tiles, 0) // per_seq)


def _mod_vec(mod_ref, row, k):
    return mod_ref[0, pl.ds(row, 1), pl.ds(k * D_MODEL, D_MODEL)]


def _mod_kernel(ct_ref, w_ref, b_ref, o_ref):
    ct = ct_ref[...]
    s = _silu(ct)
    w = w_ref[0]
    b = b_ref[0]
    o_ref[0] = jnp.zeros(o_ref.shape[1:], F32)
    for r in range(1 + DEC_BATCH):
        o_ref[0, r:r + 1, :] = jnp.sum(s[:, r:r + 1] * w, axis=0, keepdims=True) + b


def _modulation(cond_t, w_mod, b_mod):
    tn = 1024
    n_cols = N_MOD * D_MODEL
    return pl.pallas_call(
        _mod_kernel,
        grid=(DEPTH, n_cols // tn),
        in_specs=[
            pl.BlockSpec((D_MODEL, 8), lambda l, j: (0, 0)),
            pl.BlockSpec((1, D_MODEL, tn), lambda l, j: (l, 0, j)),
            pl.BlockSpec((1, 1, tn), lambda l, j: (l, 0, j)),
        ],
        out_specs=pl.BlockSpec((1, 8, tn), lambda l, j: (l, 0, j)),
        out_shape=jax.ShapeDtypeStruct((DEPTH, 8, n_cols), F32),
        compiler_params=_cparams(("arbitrary", "arbitrary")),
        name="modulation",
    )(cond_t, w_mod, b_mod.reshape(DEPTH, 1, n_cols))


def _ffn_kernel(x_ref, mod_ref, gpre_ref, gpost_ref, wi_ref, wo_ref, o_ref, acc_ref, *, tm, k0):
    row = _mod_row(pl.program_id(0), tm)
    x = x_ref[...]
    sh = _mod_vec(mod_ref, row, k0)
    sc = _mod_vec(mod_ref, row, k0 + 1)
    gt = _mod_vec(mod_ref, row, k0 + 2)
    h = (_rms(x, gpre_ref[0, 0]) * (1.0 + sc) + sh).astype(BF16)
    for c in range(N_FF_CHUNKS):
        lo = c * FF_CHUNK
        g = jnp.dot(h, wi_ref[0, 0, :, lo:lo + FF_CHUNK], preferred_element_type=F32)
        u = jnp.dot(h, wi_ref[0, 0, :, D_FF + lo:D_FF + lo + FF_CHUNK], preferred_element_type=F32)
        a = (_silu(g) * u).astype(BF16)
        part = jnp.dot(a, wo_ref[0, 0, lo:lo + FF_CHUNK, :], preferred_element_type=F32)
        if c == 0:
            acc_ref[...] = part
        else:
            acc_ref[...] += part
    o_ref[...] = x + (FFN_RES * gt) * _rms(acc_ref[...], gpost_ref[0, 0])


def _ffn(x, mod, norm_g, w_ffn_in, w_ffn_out, l, f):
    tm = TM_FFN

    def lf_block(shape, k):
        return pl.BlockSpec((1, 1) + shape, lambda i: (l, k, 0, 0), pipeline_mode=pl.Buffered(1))

    return pl.pallas_call(
        functools.partial(_ffn_kernel, tm=tm, k0=6 * f),
        grid=(T_ALL // tm,),
        in_specs=[
            pl.BlockSpec((tm, D_MODEL), lambda i: (i, 0)),
            _layer_block(l, (8, N_MOD * D_MODEL)),
            lf_block((1, D_MODEL), 4 * f),
            lf_block((1, D_MODEL), 4 * f + 1),
            lf_block((D_MODEL, 2 * D_FF), f),
            lf_block((D_FF, D_MODEL), f),
        ],
        out_specs=pl.BlockSpec((tm, D_MODEL), lambda i: (i, 0)),
        out_shape=jax.ShapeDtypeStruct((T_ALL, D_MODEL), F32),
        scratch_shapes=[pltpu.VMEM((tm, D_MODEL), F32)],
        compiler_params=_cparams(("arbitrary",)),
        name="ffn",
    )(x, mod, norm_g, norm_g, w_ffn_in, w_ffn_out)


def _proj_kernel(x_ref, mod_ref, g_ref, wa_ref, wb_ref, wdt_ref, wdt_t_ref, dtb_row_ref, dtb_col_ref,
                 z_ref, xbc_ref, v_ref, pool_ref, gate_ref, dt_ref, dt_t_ref):
    row = _mod_row(pl.program_id(0), TILE)
    sh = _mod_vec(mod_ref, row, 3)
    sc = _mod_vec(mod_ref, row, 4)
    h = (_rms(x_ref[...], g_ref[0, 0]) * (1.0 + sc) + sh).astype(BF16)

    def mm(w_ref, lo, hi):
        return jnp.dot(h, w_ref[0, :, lo:hi], preferred_element_type=F32)

    z_ref[...] = _silu(mm(wa_ref, 0, SSD_D_INNER))
    xbc_ref[...] = mm(wa_ref, SSD_D_INNER, N_WA)
    ag = mm(wb_ref, 0, PB_POOL)
    v_ref[...] = ag[:, :CONF_D] * jax.nn.sigmoid(ag[:, CONF_D:])
    pool_ref[...] = mm(wb_ref, PB_POOL, PB_GATE)
    gate_ref[...] = jax.nn.sigmoid(mm(wb_ref, PB_GATE, N_WB))
    dt_ref[...] = _softplus(mm(wdt_ref, 0, 2 * DT_LANES) + dtb_row_ref[0])
    dt_t = lax.dot_general(wdt_t_ref[0], h, (((1,), (1,)), ((), ())), preferred_element_type=F32)
    dt_t_ref[...] = _softplus(dt_t + dtb_col_ref[0])


def _proj(x, mod, norm_g, wa, wb, wdt, wdt_t, dtb_row, dtb_col, l):
    def tile(n):
        return pl.BlockSpec((TILE, n), lambda i: (i, 0))

    widths = (SSD_D_INNER, SSD_XBC, CONF_D, POOL_D, 3 * D_MODEL, 2 * DT_LANES)
    return pl.pallas_call(
        _proj_kernel,
        grid=(N_TILES,),
        in_specs=[
            tile(D_MODEL),
            _layer_block(l, (8, N_MOD * D_MODEL)),
            pl.BlockSpec((1, 1, 1, D_MODEL), lambda i: (l, 2, 0, 0), pipeline_mode=pl.Buffered(1)),
            _layer_block(l, (D_MODEL, N_WA)),
            _layer_block(l, (D_MODEL, N_WB)),
            _layer_block(l, (D_MODEL, 2 * DT_LANES)),
            _layer_block(l, (2 * SSD_HEADS, D_MODEL)),
            _layer_block(l, (1, 2 * DT_LANES)),
            _layer_block(l, (2 * SSD_HEADS, 1)),
        ],
        out_specs=[tile(n) for n in widths] + [pl.BlockSpec((2 * SSD_HEADS, TILE), lambda i: (0, i))],
        out_shape=[jax.ShapeDtypeStruct((T_ALL, n), F32) for n in widths]
        + [jax.ShapeDtypeStruct((2 * SSD_HEADS, T_ALL), F32)],
        compiler_params=_cparams(("arbitrary",)),
        name="in_proj",
    )(x, mod, norm_g, wa, wb, wdt, wdt_t, dtb_row, dtb_col)


def _fill_padded(pad_ref, cur_ref, prev_ref, next_ref, has_prev, has_next):
    pad_ref[0:HALO, :] = jnp.where(has_prev, prev_ref[...], 0.0)
    pad_ref[HALO:HALO + TILE, :] = cur_ref[...]
    pad_ref[HALO + TILE:, :] = jnp.where(has_next, next_ref[...], 0.0)


CONF_SHIFT_ROWS = TILE + 3 * 8


def _local_kernel(xc_ref, xp_ref, xn_ref, vc_ref, vp_ref, vn_ref, pc_ref, pp_ref, pn_ref,
                  scw_ref, scb_ref, ccw_ref, ccb_ref, lng_ref, lnb_ref, wbc_ref,
                  pw_ref, psc_ref, wbp_ref,
                  xo_ref, bc_ref, bp_ref,
                  xpad_ref, vpad_ref, vsh_ref, ppad_ref, cv_ref, mix_ref):
    i = pl.program_id(0)
    lat = i >= CTX_TILES
    k = jnp.maximum(i - CTX_TILES, 0) % LAT_TILES
    has_prev = jnp.logical_and(lat, k != 0)
    has_next = jnp.logical_and(lat, k != LAT_TILES - 1)
    _fill_padded(xpad_ref, xc_ref, xp_ref, xn_ref, has_prev, has_next)
    _fill_padded(vpad_ref, vc_ref, vp_ref, vn_ref, has_prev, has_next)
    _fill_padded(ppad_ref, pc_ref, pp_ref, pn_ref, has_prev, has_next)

    def ssd_cols(cb, carry):
        col = pl.ds(pl.multiple_of(cb * 128, 128), 128)
        first = HALO - SSD_CONV // 2
        acc = scb_ref[0, :, col]
        for t in range(SSD_CONV):
            acc = acc + scw_ref[0, pl.ds(t, 1), col] * xpad_ref[pl.ds(first + t, TILE), col]
        xo_ref[:, col] = _silu(acc)
        return carry

    lax.fori_loop(0, SSD_XBC // 128, ssd_cols, 0)

    for s in range(1, 8):
        vsh_ref[s - 1] = vpad_ref[s:s + CONF_SHIFT_ROWS, :]

    def conf_cols(cb, carry):
        col = pl.ds(pl.multiple_of(cb * 128, 128), 128)
        first = HALO - CONF_KERNEL // 2
        acc = ccb_ref[0, :, col]
        for t in range(CONF_KERNEL):
            off = first + t
            a, s = off // 8, off % 8
            src = vpad_ref if s == 0 else vsh_ref.at[s - 1]
            acc = acc + ccw_ref[0, pl.ds(t, 1), col] * src[pl.ds(8 * a, TILE), col]
        cv_ref[:, col] = acc
        return carry

    lax.fori_loop(0, CONF_D // 128, conf_cols, 0)
    cv = cv_ref[...]
    mu = jnp.mean(cv, axis=-1, keepdims=True)
    cen = cv - mu
    var = jnp.mean(cen * cen, axis=-1, keepdims=True)
    ln = cen * lax.rsqrt(var + EPS) * lng_ref[0] + lnb_ref[0]
    bc_ref[...] = jnp.dot(_silu(ln).astype(BF16), wbc_ref[0], preferred_element_type=F32)

    seq_len = jnp.where(lat, DEC_SEQ, SEQ)
    pos = k * TILE + lax.broadcasted_iota(jnp.int32, (TILE, 1), 0)
    for gi, w in enumerate(POOL_WINDOWS):
        cols = slice(gi * 128, (gi + 1) * 128)
        first = HALO - w // 2
        s = ppad_ref[first:first + TILE, cols]
        for j in range(1, w):
            s = s + ppad_ref[first + j:first + j + TILE, cols]
        lo = jnp.maximum(pos - w // 2, 0)
        hi = jnp.minimum(pos - w // 2 + w, seq_len)
        pooled = s / (hi - lo).astype(F32) - pc_ref[:, cols]
        mixed = jnp.dot(pooled.astype(BF16), pw_ref[0, gi], preferred_element_type=F32)
        mix_ref[:, cols] = mixed * psc_ref[0, :, cols]
    bp_ref[...] = jnp.dot(mix_ref[...].astype(BF16), wbp_ref[0], preferred_element_type=F32)


def _local(xbc, v, pool, scw, scb, ccw, ccb, lng, lnb, wbc, pw, psc, wbp, l):
    per_tile = TILE // HALO
    n_halo = T_ALL // HALO

    def cur(n):
        return pl.BlockSpec((TILE, n), lambda i: (i, 0))

    def prev(n):
        return pl.BlockSpec((HALO, n), lambda i: (jnp.maximum(i * per_tile - 1, 0), 0))

    def nxt(n):
        return pl.BlockSpec((HALO, n), lambda i: (jnp.minimum((i + 1) * per_tile, n_halo - 1), 0))

    def trio(n):
        return [cur(n), prev(n), nxt(n)]

    return pl.pallas_call(
        _local_kernel,
        grid=(N_TILES,),
        in_specs=trio(SSD_XBC) + trio(CONF_D) + trio(POOL_D) + [
            _layer_block(l, (SSD_CONV, SSD_XBC)), _layer_block(l, (1, SSD_XBC)),
            _layer_block(l, (CONF_KERNEL, CONF_D)), _layer_block(l, (1, CONF_D)),
            _layer_block(l, (1, CONF_D)), _layer_block(l, (1, CONF_D)),
            _layer_block(l, (CONF_D, D_MODEL)),
            _layer_block(l, (len(POOL_WINDOWS), 128, 128)), _layer_block(l, (1, POOL_D)),
            _layer_block(l, (POOL_D, D_MODEL)),
        ],
        out_specs=[cur(SSD_XBC), cur(D_MODEL), cur(D_MODEL)],
        out_shape=[jax.ShapeDtypeStruct((T_ALL, SSD_XBC), F32),
                   jax.ShapeDtypeStruct((T_ALL, D_MODEL), F32),
                   jax.ShapeDtypeStruct((T_ALL, D_MODEL), F32)],
        scratch_shapes=[pltpu.VMEM((TILE + 2 * HALO, SSD_XBC), F32),
                        pltpu.VMEM((TILE + 2 * HALO, CONF_D), F32),
                        pltpu.VMEM((7, CONF_SHIFT_ROWS, CONF_D), F32),
                        pltpu.VMEM((TILE + 2 * HALO, POOL_D), F32),
                        pltpu.VMEM((TILE, CONF_D), F32),
                        pltpu.VMEM((TILE, POOL_D), F32)],
        compiler_params=_cparams(("arbitrary",)),
        name="local_branches",
    )(xbc, xbc, xbc, v, v, v, pool, pool, pool, scw, scb, ccw, ccb, lng, lnb, wbc, pw, psc, wbp)


_NT = (((1,), (1,)), ((), ()))


def _split3(v):
    p0 = v.astype(BF16)
    r = v - p0.astype(F32)
    p1 = r.astype(BF16)
    p2 = (r - p1.astype(F32)).astype(BF16)
    return p0, p1, p2


def _scan_kernel(*refs, nc, latent):
    if latent:
        (xc_ref, dt_ref, dtt_ref, tri_ref, alr_ref, alc_ref, dsk_ref, h0_ref,
         y_ref, xt_ref, acs_ref, acst_ref, st_ref, s_ref, yt_ref) = refs
    else:
        (xc_ref, dt_ref, dtt_ref, tri_ref, alr_ref, alc_ref, dsk_ref, _,
         y_ref, ns_ref, xt_ref, acs_ref, acst_ref, st_ref, s_ref, yt_ref) = refs

    def rows_of(c):
        return pl.ds(pl.multiple_of(c * SSD_CHUNK, SSD_CHUNK), SSD_CHUNK)

    def dir_rows(d):
        return slice(d * SSD_HEADS, (d + 1) * SSD_HEADS)

    def b_cols(g):
        lo = SSD_D_INNER + g * SSD_D_STATE
        return slice(lo, lo + SSD_D_STATE)

    def c_cols(g):
        lo = SSD_D_INNER + SSD_GROUPS * SSD_D_STATE + g * SSD_D_STATE
        return slice(lo, lo + SSD_D_STATE)

    def head_rows(h):
        return slice(h * SSD_HEAD_DIM, (h + 1) * SSD_HEAD_DIM)

    def total_col(acs_t, d):
        return acs_t[:, SSD_CHUNK - 1:] if d == 0 else acs_t[:, :1]

    def cumsums(c, carry):
        rows = rows_of(c)
        for d in range(2):
            tri = tri_ref[d]
            a_row = -jnp.exp(alr_ref[0, d])
            a_col = -jnp.exp(alc_ref[0, d])
            p = jnp.concatenate(_split3(dt_ref[rows, d * DT_LANES:(d + 1) * DT_LANES]), axis=1)
            r = jnp.dot(tri, p, preferred_element_type=F32)
            acs_ref[d, rows, :] = (r[:, :128] + r[:, 128:256] + r[:, 256:]) * a_row
            q = jnp.concatenate(_split3(dtt_ref[dir_rows(d), rows]), axis=0)
            rt = lax.dot_general(q, tri, _NT, preferred_element_type=F32)
            acst_ref[d, :, rows] = (rt[:SSD_HEADS] + rt[SSD_HEADS:2 * SSD_HEADS] + rt[2 * SSD_HEADS:]) * a_col
        return carry

    lax.fori_loop(0, nc, cumsums, 0)

    def local_states(c, carry):
        rows = rows_of(c)
        xt = xc_ref[rows, 0:SSD_D_INNER].T
        xt_ref[:, rows] = xt
        for d in range(2):
            acs_t = acst_ref[d, :, rows]
            w = dtt_ref[dir_rows(d), rows] * jnp.exp(total_col(acs_t, d) - acs_t)
            for g in range(SSD_GROUPS):
                bg = xc_ref[rows, b_cols(g)].astype(BF16)
                parts = [(xt[head_rows(h), :] * w[h:h + 1, :]).astype(BF16)
                         for h in range(g * HEADS_PER_GROUP, (g + 1) * HEADS_PER_GROUP)]
                st_ref[d, c, g * GROUP_ROWS:(g + 1) * GROUP_ROWS, :] = jnp.dot(
                    jnp.concatenate(parts, axis=0), bg, preferred_element_type=F32)
        return carry

    lax.fori_loop(0, nc, local_states, 0)

    for d in range(2):
        if latent:
            s_ref[...] = h0_ref[0, 0, d]
        else:
            s_ref[...] = jnp.zeros(s_ref.shape, F32)

        def recur(j, carry, d=d):
            c = j if d == 0 else nc - 1 - j
            acs_t = acst_ref[d, :, rows_of(c)]
            e_tot = jnp.broadcast_to(jnp.exp(total_col(acs_t, d)), (SSD_HEADS, SSD_D_STATE))
            for h in range(SSD_HEADS):
                s_old = s_ref[head_rows(h), :]
                cs = st_ref[d, c, head_rows(h), :]
                st_ref[d, c, head_rows(h), :] = s_old
                s_ref[head_rows(h), :] = s_old * e_tot[h:h + 1, :] + cs
            return carry

        lax.fori_loop(0, nc, recur, 0)
        if not latent:
            ns_ref[0, 0, d] = s_ref[...]

    s_idx = lax.broadcasted_iota(jnp.int32, (SSD_CHUNK, SSD_CHUNK), 0)
    l_idx = lax.broadcasted_iota(jnp.int32, (SSD_CHUNK, SSD_CHUNK), 1)
    visible = (s_idx <= l_idx, s_idx >= l_idx)

    def outputs(c, carry):
        rows = rows_of(c)
        acs_t = [acst_ref[d, :, rows] for d in range(2)]
        e_acs_t = [jnp.exp(a) for a in acs_t]
        dt_t = [dtt_ref[dir_rows(d), rows] for d in range(2)]
        for g in range(SSD_GROUPS):
            bg = xc_ref[rows, b_cols(g)].astype(BF16)
            cg = xc_ref[rows, c_cols(g)].astype(BF16)
            g_t = lax.dot_general(bg, cg, _NT, preferred_element_type=F32)
            y_in = [lax.dot_general(st_ref[d, c, g * GROUP_ROWS:(g + 1) * GROUP_ROWS, :].astype(BF16), cg, _NT,
                                    preferred_element_type=F32) for d in range(2)]
            for hg in range(HEADS_PER_GROUP):
                h = g * HEADS_PER_GROUP + hg
                x_h = xt_ref[head_rows(h), rows]
                lhs, rhs = [], []
                y_h = dsk_ref[0, head_rows(h), :] * x_h
                for d in range(2):
                    seg = jnp.where(visible[d], acs_t[d][h:h + 1, :] - acs_ref[d, rows, h:h + 1], -jnp.inf)
                    rhs.append((g_t * jnp.exp(seg)).astype(BF16))
                    lhs.append((x_h * dt_t[d][h:h + 1, :]).astype(BF16))
                    y_h = y_h + y_in[d][hg * SSD_HEAD_DIM:(hg + 1) * SSD_HEAD_DIM, :] * e_acs_t[d][h:h + 1, :]
                y_h = y_h + jnp.dot(jnp.concatenate(lhs, axis=1), jnp.concatenate(rhs, axis=0),
                                    preferred_element_type=F32)
                yt_ref[head_rows(h), :] = y_h
        y_ref[rows, :] = yt_ref[...].T
        return carry

    lax.fori_loop(0, nc, outputs, 0)


def _scan(xconv, dt, dt_t, tri, alr, alc, dskip, state_ssd5, ns_all, l, latent):
    seq = DEC_SEQ if latent else SEQ
    n_seq = DEC_BATCH if latent else BATCH
    first = T_CTX // seq if latent else 0
    nc = seq // SSD_CHUNK
    state_block = (1, 1, 2, SSD_D_INNER, SSD_D_STATE)
    in_specs = [
        pl.BlockSpec((seq, SSD_XBC), lambda i: (first + i, 0)),
        pl.BlockSpec((seq, 2 * DT_LANES), lambda i: (first + i, 0)),
        pl.BlockSpec((2 * SSD_HEADS, seq), lambda i: (0, first + i)),
        pl.BlockSpec((2, SSD_CHUNK, SSD_CHUNK), lambda i: (0, 0, 0), pipeline_mode=pl.Buffered(1)),
        _layer_block(l, (2, 1, DT_LANES)),
        _layer_block(l, (2, SSD_HEADS, 1)),
        _layer_block(l, (SSD_D_INNER, SSD_D_STATE)),
    ]
    scratch = [pltpu.VMEM((SSD_D_INNER, seq), F32),
               pltpu.VMEM((2, seq, DT_LANES), F32),
               pltpu.VMEM((2, SSD_HEADS, seq), F32),
               pltpu.VMEM((2, nc, SSD_D_INNER, SSD_D_STATE), F32),
               pltpu.VMEM((SSD_D_INNER, SSD_D_STATE), F32),
               pltpu.VMEM((SSD_D_INNER, SSD_CHUNK), F32)]
    kern = functools.partial(_scan_kernel, nc=nc, latent=latent)
    if latent:
        return pl.pallas_call(
            kern, grid=(n_seq,),
            in_specs=in_specs + [pl.BlockSpec(state_block, lambda i: (i, l, 0, 0, 0))],
            out_specs=pl.BlockSpec((seq, SSD_D_INNER), lambda i: (i, 0)),
            out_shape=jax.ShapeDtypeStruct((T_LAT, SSD_D_INNER), F32),
            scratch_shapes=scratch,
            compiler_params=_cparams(("arbitrary",)),
            name="ssd_scan_latent",
        )(xconv, dt, dt_t, tri, alr, alc, dskip, state_ssd5)
    return pl.pallas_call(
        kern, grid=(n_seq,),
        in_specs=in_specs + [pl.BlockSpec(memory_space=pl.ANY)],
        out_specs=[pl.BlockSpec((seq, SSD_D_INNER), lambda i: (i, 0)),
                   pl.BlockSpec(state_block, lambda i: (i, l, 0, 0, 0))],
        out_shape=[jax.ShapeDtypeStruct((T_CTX, SSD_D_INNER), F32),
                   jax.ShapeDtypeStruct(ns_all.shape, F32)],
        scratch_shapes=scratch,
        input_output_aliases={7: 1},
        compiler_params=_cparams(("arbitrary",)),
        name="ssd_scan_ctx",
    )(xconv, dt, dt_t, tri, alr, alc, dskip, ns_all)


def _merge_kernel(x_ref, mod_ref, yc_ref, yl_ref, z_ref, gate_ref, bc_ref, bp_ref, sng_ref, wbs_ref, wout_ref,
                  gpost_ref, o_ref):
    i = pl.program_id(0)
    row = _mod_row(i, TILE)
    gt = _mod_vec(mod_ref, row, 5)
    y = jnp.where(i < CTX_TILES, yc_ref[...], yl_ref[...])
    ys = y * z_ref[...]
    br_ssd = jnp.dot(_rms(ys, sng_ref[0]).astype(BF16), wbs_ref[0], preferred_element_type=F32)
    merged = (gate_ref[:, 0:D_MODEL] * br_ssd
              + gate_ref[:, D_MODEL:2 * D_MODEL] * bc_ref[...]
              + gate_ref[:, 2 * D_MODEL:] * bp_ref[...])
    yo = jnp.dot(merged.astype(BF16), wout_ref[0], preferred_element_type=F32)
    o_ref[...] = x_ref[...] + gt * _rms(yo, gpost_ref[0, 0])


def _merge(x, mod, y_ctx, y_lat, z, gate, bc, bp, sng, wbs, wout, norm_g, l):
    def tile(n):
        return pl.BlockSpec((TILE, n), lambda i: (i, 0))

    return pl.pallas_call(
        _merge_kernel,
        grid=(N_TILES,),
        in_specs=[
            tile(D_MODEL),
            _layer_block(l, (8, N_MOD * D_MODEL)),
            pl.BlockSpec((TILE, SSD_D_INNER), lambda i: (jnp.minimum(i, CTX_TILES - 1), 0)),
            pl.BlockSpec((TILE, SSD_D_INNER), lambda i: (jnp.maximum(i - CTX_TILES, 0), 0)),
            tile(SSD_D_INNER), tile(3 * D_MODEL), tile(D_MODEL), tile(D_MODEL),
            _layer_block(l, (1, SSD_D_INNER)),
            _layer_block(l, (SSD_D_INNER, D_MODEL)),
            _layer_block(l, (D_MODEL, D_MODEL)),
            pl.BlockSpec((1, 1, 1, D_MODEL), lambda i: (l, 3, 0, 0), pipeline_mode=pl.Buffered(1)),
        ],
        out_specs=tile(D_MODEL),
        out_shape=jax.ShapeDtypeStruct((T_ALL, D_MODEL), F32),
        compiler_params=_cparams(("arbitrary",)),
        name="merge_out",
    )(x, mod, y_ctx, y_lat, z, gate, bc, bp, sng, wbs, wout, norm_g)


def _pos_embed_2d(n_tokens):
    rows = n_tokens // GRID_W
    r, col = jnp.meshgrid(jnp.arange(rows), jnp.arange(GRID_W), indexing='ij')
    r = r.reshape(-1).astype(F32)
    col = col.reshape(-1).astype(F32)
    q = D_MODEL // 4
    omega = 1.0 / (10000.0 ** (jnp.arange(q, dtype=F32) / q))
    ar = r[:, None] * omega
    ac = col[:, None] * omega
    return jnp.concatenate([jnp.sin(ar), jnp.cos(ar), jnp.sin(ac), jnp.cos(ac)], axis=-1)


def _pad_lanes(a, n):
    return jnp.pad(a, [(0, 0)] * (a.ndim - 1) + [(0, n - a.shape[-1])])


def kernel(x_prompt, x_sample, state_ssd, c, c_ctx, w_mod, b_mod, norm_g, w_ffn_in, w_ffn_out, w_in,
           ssd_conv_w, ssd_conv_b, ssd_a_log, ssd_dt_bias, ssd_d, ssd_norm_g, w_br_ssd, conf_conv_w,
           conf_conv_b, conf_ln_g, conf_ln_b, w_br_conf, pool_w, pool_scale, w_br_pool, w_out):
    xs = x_sample + _pos_embed_2d(DEC_SEQ).astype(x_sample.dtype)[None]
    x = jnp.concatenate([x_prompt.reshape(T_CTX, D_MODEL), xs.reshape(T_LAT, D_MODEL)], axis=0)

    cond_t = jnp.concatenate([c_ctx[None, :], c, jnp.zeros((8 - 1 - DEC_BATCH, D_MODEL), F32)], axis=0).T
    mod = _modulation(cond_t, w_mod, b_mod)

    wi_b = w_ffn_in.astype(BF16)
    wo_b = w_ffn_out.astype(BF16)
    wa_b = w_in[:, :, :OFF_XBC].astype(BF16)
    wb_b = w_in[:, :, OFF_DT:].astype(BF16)
    w_dt = w_in[:, :, OFF_XBC:OFF_DT]
    wdt_b = jnp.concatenate([_pad_lanes(w_dt[:, :, :SSD_HEADS], DT_LANES),
                             _pad_lanes(w_dt[:, :, SSD_HEADS:], DT_LANES)], axis=-1).astype(BF16)
    wdt_t_b = jnp.swapaxes(w_dt, 1, 2).astype(BF16)
    dtb_row = _pad_lanes(ssd_dt_bias, DT_LANES).reshape(DEPTH, 1, 2 * DT_LANES)
    dtb_col = ssd_dt_bias.reshape(DEPTH, 2 * SSD_HEADS, 1)
    wbs_b = w_br_ssd.astype(BF16)
    wbc_b = w_br_conf.astype(BF16)
    wbp_b = w_br_pool.astype(BF16)
    wout_b = w_out.astype(BF16)
    pw_b = pool_w.astype(BF16)
    ng = norm_g.reshape(DEPTH, 6, 1, D_MODEL)
    alr = _pad_lanes(ssd_a_log, DT_LANES).reshape(DEPTH, 2, 1, DT_LANES)
    alc = ssd_a_log.reshape(DEPTH, 2, SSD_HEADS, 1)
    dskip = jnp.broadcast_to(jnp.repeat(ssd_d, SSD_HEAD_DIM, axis=1)[:, :, None],
                             (DEPTH, SSD_D_INNER, SSD_D_STATE))
    idx = jnp.arange(SSD_CHUNK)
    tri = jnp.stack([idx[:, None] >= idx[None, :], idx[:, None] <= idx[None, :]]).astype(BF16)
    state5 = state_ssd.reshape(DEC_BATCH, DEPTH, 2, SSD_D_INNER, SSD_D_STATE)
    ns_all = jnp.zeros((BATCH, DEPTH, 2, SSD_D_INNER, SSD_D_STATE), F32)

    def row1(a):
        return a.reshape(DEPTH, 1, a.shape[-1])

    for l in range(DEPTH):
        x = _ffn(x, mod, ng, wi_b, wo_b, l, 0)
        z, xbc, v, pool, gate, dt, dt_t = _proj(x, mod, ng, wa_b, wb_b, wdt_b, wdt_t_b, dtb_row, dtb_col, l)
        xconv, br_conf, br_pool = _local(
            xbc, v, pool, ssd_conv_w, row1(ssd_conv_b), conf_conv_w, row1(conf_conv_b),
            row1(conf_ln_g), row1(conf_ln_b), wbc_b, pw_b, row1(pool_scale), wbp_b, l)
        y_ctx, ns_all = _scan(xconv, dt, dt_t, tri, alr, alc, dskip, None, ns_all, l, latent=False)
        y_lat = _scan(xconv, dt, dt_t, tri, alr, alc, dskip, state5, None, l, latent=True)
        x = _merge(x, mod, y_ctx, y_lat, z, gate, br_conf, br_pool, row1(ssd_norm_g), wbs_b, wout_b, ng, l)
        x = _ffn(x, mod, ng, wi_b, wo_b, l, 1)

    y_prompt = x[:T_CTX].reshape(BATCH, SEQ, D_MODEL)
    y_sample = x[T_CTX:].reshape(DEC_BATCH, DEC_SEQ, D_MODEL)
    new_state = ns_all.reshape(BATCH, DEPTH, 2, SSD_HEADS, SSD_HEAD_DIM, SSD_D_STATE).astype(x_prompt.dtype)
    return (y_prompt, y_sample, new_state)
```

```python
import functools

import jax
import jax.numpy as jnp
from jax import lax
from jax.experimental import pallas as pl
from jax.experimental.pallas import tpu as pltpu

F32 = jnp.float32
BF16 = jnp.bfloat16

D_MODEL = 1024
BATCH = 32
SEQ = 256
DEPTH = 4
DEC_BATCH = 2
DEC_SEQ = 1024
GRID_W = 64
SSD_D_INNER = 1024
SSD_HEAD_DIM = 64
SSD_HEADS = 16
SSD_GROUPS = 4
SSD_D_STATE = 128
SSD_CONV = 5
SSD_CHUNK = 128
SSD_XBC = 2048
CONF_D = 512
CONF_KERNEL = 31
POOL_D = 512
POOL_WINDOWS = (2, 4, 8, 16)
D_FF = 2816
N_MOD = 9
FFN_RES = 0.5
EPS = 1e-6
OFF_XBC = 3072
OFF_DT = 3104
IN_COLS = 7712

T_CTX = BATCH * SEQ
T_LAT = DEC_BATCH * DEC_SEQ
T_ALL = T_CTX + T_LAT
TILE = 256
HALO = 16
N_TILES = T_ALL // TILE
CTX_TILES = T_CTX // TILE
LAT_TILES = DEC_SEQ // TILE
HEADS_PER_GROUP = SSD_HEADS // SSD_GROUPS
GROUP_ROWS = HEADS_PER_GROUP * SSD_HEAD_DIM
BC_COLS = 2 * SSD_GROUPS * SSD_D_STATE
CONV_PAD = 8

FF_CHUNK = 256
N_FF_CHUNKS = D_FF // FF_CHUNK
TM_FFN = 512

N_WA = OFF_XBC
N_WB = IN_COLS - OFF_DT
PB_POOL = 2 * CONF_D
PB_GATE = PB_POOL + POOL_D
DT_LANES = 128

VMEM_LIMIT = 56 * 1024 * 1024


def _cparams(sem):
    return pltpu.CompilerParams(dimension_semantics=sem, vmem_limit_bytes=VMEM_LIMIT)


def _layer_block(l, shape):
    nd = len(shape)
    return pl.BlockSpec((1,) + tuple(shape), lambda *_: (l,) + (0,) * nd, pipeline_mode=pl.Buffered(1))


def _rms(x, g):
    ms = jnp.mean(x * x, axis=-1, keepdims=True)
    return x * lax.rsqrt(ms + EPS) * g


def _sigmoid(x):
    return 0.5 * jnp.tanh(0.5 * x) + 0.5


def _silu(x):
    u = 0.5 * x
    return u * jnp.tanh(u) + u


def _softplus(x):
    return jnp.maximum(x, 0.0) + jnp.log(1.0 + jnp.exp(-jnp.abs(x)))


def _mod_row(i, tm):
    ctx_tiles = T_CTX // tm
    per_seq = DEC_SEQ // tm
    return jnp.where(i < ctx_tiles, 0, 1 + jnp.maximum(i - ctx_tiles, 0) // per_seq)


def _mod_vec(mod_ref, row, k):
    return mod_ref[0, pl.ds(row, 1), pl.ds(k * D_MODEL, D_MODEL)]


def _mod_kernel(ct_ref, w_ref, b_ref, o_ref):
    ct = ct_ref[...]
    s = _silu(ct)
    w = w_ref[0]
    b = b_ref[0]
    o_ref[0] = jnp.zeros(o_ref.shape[1:], F32)
    for r in range(1 + DEC_BATCH):
        o_ref[0, r:r + 1, :] = jnp.sum(s[:, r:r + 1] * w, axis=0, keepdims=True) + b


def _modulation(cond_t, w_mod, b_mod):
    tn = 1024
    n_cols = N_MOD * D_MODEL
    return pl.pallas_call(
        _mod_kernel,
        grid=(DEPTH, n_cols // tn),
        in_specs=[
            pl.BlockSpec((D_MODEL, 8), lambda l, j: (0, 0)),
            pl.BlockSpec((1, D_MODEL, tn), lambda l, j: (l, 0, j)),
            pl.BlockSpec((1, 1, tn), lambda l, j: (l, 0, j)),
        ],
        out_specs=pl.BlockSpec((1, 8, tn), lambda l, j: (l, 0, j)),
        out_shape=jax.ShapeDtypeStruct((DEPTH, 8, n_cols), F32),
        compiler_params=_cparams(("arbitrary", "arbitrary")),
        name="modulation",
    )(cond_t, w_mod, b_mod.reshape(DEPTH, 1, n_cols))


def _ffn_kernel(*refs, tm, k0, split_in, split_out):
    refs = list(refs)
    x_refs = [refs.pop(0) for _ in range(2 if split_in else 1)]
    mod_ref, gpre_ref, gpost_ref, wi_ref, wo_ref = refs[:5]
    o_refs = refs[5:-1]
    acc_ref = refs[-1]
    i = pl.program_id(0)
    is_ctx = i < T_CTX // tm
    row = _mod_row(i, tm)
    x = jnp.where(is_ctx, x_refs[0][...], x_refs[1][...]) if split_in else x_refs[0][...]
    sh = _mod_vec(mod_ref, row, k0)
    sc = _mod_vec(mod_ref, row, k0 + 1)
    gt = _mod_vec(mod_ref, row, k0 + 2)
    h = (_rms(x, gpre_ref[0, 0]) * (1.0 + sc) + sh).astype(BF16)
    for c in range(N_FF_CHUNKS):
        lo = c * FF_CHUNK
        g = jnp.dot(h, wi_ref[0, 0, :, lo:lo + FF_CHUNK], preferred_element_type=F32)
        u = jnp.dot(h, wi_ref[0, 0, :, D_FF + lo:D_FF + lo + FF_CHUNK], preferred_element_type=F32)
        a = (_silu(g) * u).astype(BF16)
        part = jnp.dot(a, wo_ref[0, 0, lo:lo + FF_CHUNK, :], preferred_element_type=F32)
        if c == 0:
            acc_ref[...] = part
        else:
            acc_ref[...] += part
    out = x + (FFN_RES * gt) * _rms(acc_ref[...], gpost_ref[0, 0])
    if split_out:
        @pl.when(is_ctx)
        def _():
            o_refs[0][...] = out

        @pl.when(jnp.logical_not(is_ctx))
        def _():
            o_refs[1][...] = out
    else:
        o_refs[0][...] = out


def _ffn(xs, mod, norm_g, w_ffn_in, w_ffn_out, l, f, split_out=False):
    tm = TM_FFN
    ctx_tiles = T_CTX // tm
    split_in = isinstance(xs, tuple)

    def lf_block(shape, k):
        return pl.BlockSpec((1, 1) + shape, lambda i: (l, k, 0, 0), pipeline_mode=pl.Buffered(1))

    merged = pl.BlockSpec((tm, D_MODEL), lambda i: (i, 0))
    ctx_part = pl.BlockSpec((tm, D_MODEL), lambda i: (jnp.minimum(i, ctx_tiles - 1), 0))
    lat_part = pl.BlockSpec((tm, D_MODEL), lambda i: (jnp.maximum(i - ctx_tiles, 0), 0))
    parts_shape = [jax.ShapeDtypeStruct((T_CTX, D_MODEL), F32), jax.ShapeDtypeStruct((T_LAT, D_MODEL), F32)]
    return pl.pallas_call(
        functools.partial(_ffn_kernel, tm=tm, k0=6 * f, split_in=split_in, split_out=split_out),
        grid=(T_ALL // tm,),
        in_specs=([ctx_part, lat_part] if split_in else [merged]) + [
            _layer_block(l, (8, N_MOD * D_MODEL)),
            lf_block((1, D_MODEL), 4 * f),
            lf_block((1, D_MODEL), 4 * f + 1),
            lf_block((D_MODEL, 2 * D_FF), f),
            lf_block((D_FF, D_MODEL), f),
        ],
        out_specs=[ctx_part, lat_part] if split_out else merged,
        out_shape=parts_shape if split_out else jax.ShapeDtypeStruct((T_ALL, D_MODEL), F32),
        scratch_shapes=[pltpu.VMEM((tm, D_MODEL), F32)],
        compiler_params=_cparams(("arbitrary",)),
        name="ffn",
    )(*(xs if split_in else (xs,)), mod, norm_g, norm_g, w_ffn_in, w_ffn_out)


def _proj_kernel(x_ref, mod_ref, g_ref, wa_ref, wb_ref, wdt_ref, wdt_t_ref, dtb_row_ref, dtb_col_ref,
                 z_ref, xbc_ref, v_ref, pool_ref, gate_ref, dt_ref, dt_t_ref):
    row = _mod_row(pl.program_id(0), TILE)
    sh = _mod_vec(mod_ref, row, 3)
    sc = _mod_vec(mod_ref, row, 4)
    h = (_rms(x_ref[...], g_ref[0, 0]) * (1.0 + sc) + sh).astype(BF16)

    def mm(w_ref, lo, hi):
        return jnp.dot(h, w_ref[0, :, lo:hi], preferred_element_type=F32)

    z_ref[...] = _silu(mm(wa_ref, 0, SSD_D_INNER))
    xbc_ref[...] = mm(wa_ref, SSD_D_INNER, N_WA)
    ag = mm(wb_ref, 0, PB_POOL)
    v_ref[...] = ag[:, :CONF_D] * _sigmoid(ag[:, CONF_D:])
    pool_ref[...] = mm(wb_ref, PB_POOL, PB_GATE)
    gate_ref[...] = _sigmoid(mm(wb_ref, PB_GATE, N_WB))
    dt_ref[...] = _softplus(mm(wdt_ref, 0, 2 * DT_LANES) + dtb_row_ref[0])
    dt_t = lax.dot_general(wdt_t_ref[0], h, (((1,), (1,)), ((), ())), preferred_element_type=F32)
    dt_t_ref[...] = _softplus(dt_t + dtb_col_ref[0])


def _proj(x, mod, norm_g, wa, wb, wdt, wdt_t, dtb_row, dtb_col, l):
    def tile(n):
        return pl.BlockSpec((TILE, n), lambda i: (i, 0))

    widths = (SSD_D_INNER, SSD_XBC, CONF_D, POOL_D, 3 * D_MODEL, 2 * DT_LANES)
    return pl.pallas_call(
        _proj_kernel,
        grid=(N_TILES,),
        in_specs=[
            tile(D_MODEL),
            _layer_block(l, (8, N_MOD * D_MODEL)),
            pl.BlockSpec((1, 1, 1, D_MODEL), lambda i: (l, 2, 0, 0), pipeline_mode=pl.Buffered(1)),
            _layer_block(l, (D_MODEL, N_WA)),
            _layer_block(l, (D_MODEL, N_WB)),
            _layer_block(l, (D_MODEL, 2 * DT_LANES)),
            _layer_block(l, (2 * SSD_HEADS, D_MODEL)),
            _layer_block(l, (1, 2 * DT_LANES)),
            _layer_block(l, (2 * SSD_HEADS, 1)),
        ],
        out_specs=[tile(n) for n in widths] + [pl.BlockSpec((2 * SSD_HEADS, TILE), lambda i: (0, i))],
        out_shape=[jax.ShapeDtypeStruct((T_ALL, n), F32) for n in widths]
        + [jax.ShapeDtypeStruct((2 * SSD_HEADS, T_ALL), F32)],
        compiler_params=_cparams(("arbitrary",)),
        name="in_proj",
    )(x, mod, norm_g, wa, wb, wdt, wdt_t, dtb_row, dtb_col)


_NT = (((1,), (1,)), ((), ()))


def _split3(v):
    p0 = v.astype(BF16)
    r = v - p0.astype(F32)
    p1 = r.astype(BF16)
    p2 = (r - p1.astype(F32)).astype(BF16)
    return p0, p1, p2


def _scan_kernel(*refs, nc, latent, first_layer):
    (xbc_ref, dt_ref, dtt_ref, tri_ref, alr_ref, alc_ref, dsk_ref, scw_ref, scb_ref) = refs[:9]
    refs = refs[9:]
    if latent:
        h0_ref, y_ref = refs[:2]
        refs = refs[2:]
    elif first_layer:
        y_ref, ns_ref = refs[:2]
        refs = refs[2:]
    else:
        y_ref, ns_ref = refs[1:3]
        refs = refs[3:]
    xpad_ref, xt_ref, bcs_ref, acs_ref, acst_ref, st_ref, s_ref, yt_ref = refs
    seq = nc * SSD_CHUNK

    def rows_of(c):
        return pl.ds(pl.multiple_of(c * SSD_CHUNK, SSD_CHUNK), SSD_CHUNK)

    def dir_rows(d):
        return slice(d * SSD_HEADS, (d + 1) * SSD_HEADS)

    def b_cols(g):
        return slice(g * SSD_D_STATE, (g + 1) * SSD_D_STATE)

    def c_cols(g):
        lo = SSD_GROUPS * SSD_D_STATE + g * SSD_D_STATE
        return slice(lo, lo + SSD_D_STATE)

    def head_rows(h):
        return slice(h * SSD_HEAD_DIM, (h + 1) * SSD_HEAD_DIM)

    def total_col(acs_t, d):
        return acs_t[:, SSD_CHUNK - 1:] if d == 0 else acs_t[:, :1]

    zeros_pad = jnp.zeros((CONV_PAD, SSD_XBC), F32)
    xpad_ref[0:CONV_PAD, :] = zeros_pad
    xpad_ref[CONV_PAD + seq:, :] = zeros_pad
    xpad_ref[CONV_PAD:CONV_PAD + seq, :] = xbc_ref[...]
    for ct in range(nc):
        first = CONV_PAD + ct * SSD_CHUNK - SSD_CONV // 2

        def conv_block(cb, first=first):
            col = pl.ds(pl.multiple_of(cb * 128, 128), 128)
            acc = scb_ref[0, :, col]
            for t in range(SSD_CONV):
                acc = acc + scw_ref[0, pl.ds(t, 1), col] * xpad_ref[pl.ds(first + t, SSD_CHUNK), col]
            return _silu(acc)

        def conv_x(cb, carry, ct=ct, conv_block=conv_block):
            xt_ref[pl.ds(pl.multiple_of(cb * 128, 128), 128), ct * SSD_CHUNK:(ct + 1) * SSD_CHUNK] = conv_block(cb).T
            return carry

        def conv_bc(cb, carry, ct=ct, conv_block=conv_block):
            col = pl.ds(pl.multiple_of(cb * 128, 128), 128)
            bcs_ref[ct * SSD_CHUNK:(ct + 1) * SSD_CHUNK, col] = conv_block(cb + SSD_D_INNER // 128).astype(BF16)
            return carry

        lax.fori_loop(0, SSD_D_INNER // 128, conv_x, 0, unroll=2)
        lax.fori_loop(0, BC_COLS // 128, conv_bc, 0)

    def cumsums(c, carry):
        rows = rows_of(c)
        for d in range(2):
            tri = tri_ref[d]
            a_row = -jnp.exp(alr_ref[0, d])
            a_col = -jnp.exp(alc_ref[0, d])
            p = jnp.concatenate(_split3(dt_ref[rows, d * DT_LANES:(d + 1) * DT_LANES]), axis=1)
            r = jnp.dot(tri, p, preferred_element_type=F32)
            acs_ref[d, rows, :] = (r[:, :128] + r[:, 128:256] + r[:, 256:]) * a_row
            q = jnp.concatenate(_split3(dtt_ref[dir_rows(d), rows]), axis=0)
            rt = lax.dot_general(q, tri, _NT, preferred_element_type=F32)
            acst_ref[d, :, rows] = (rt[:SSD_HEADS] + rt[SSD_HEADS:2 * SSD_HEADS] + rt[2 * SSD_HEADS:]) * a_col
        return carry

    lax.fori_loop(0, nc, cumsums, 0)

    def local_states(c, carry):
        rows = rows_of(c)
        xt = xt_ref[:, rows]
        for d in range(2):
            acs_t = acst_ref[d, :, rows]
            w = dtt_ref[dir_rows(d), rows] * jnp.exp(total_col(acs_t, d) - acs_t)
            for g in range(SSD_GROUPS):
                bg = bcs_ref[rows, b_cols(g)]
                parts = [(xt[head_rows(h), :] * w[h:h + 1, :]).astype(BF16)
                         for h in range(g * HEADS_PER_GROUP, (g + 1) * HEADS_PER_GROUP)]
                st_ref[d, c, g * GROUP_ROWS:(g + 1) * GROUP_ROWS, :] = jnp.dot(
                    jnp.concatenate(parts, axis=0), bg, preferred_element_type=F32)
        return carry

    lax.fori_loop(0, nc, local_states, 0)

    for d in range(2):
        if latent:
            s_ref[...] = h0_ref[0, 0, d]
        else:
            s_ref[...] = jnp.zeros(s_ref.shape, F32)

        def recur(j, carry, d=d):
            c = j if d == 0 else nc - 1 - j
            acs_t = acst_ref[d, :, rows_of(c)]
            e_tot = jnp.broadcast_to(jnp.exp(total_col(acs_t, d)), (SSD_HEADS, SSD_D_STATE))
            for h in range(SSD_HEADS):
                s_old = s_ref[head_rows(h), :]
                cs = st_ref[d, c, head_rows(h), :]
                st_ref[d, c, head_rows(h), :] = s_old
                s_ref[head_rows(h), :] = s_old * e_tot[h:h + 1, :] + cs
            return carry

        lax.fori_loop(0, nc, recur, 0)
        if not latent:
            ns_ref[0, 0, d] = s_ref[...]
    if first_layer and not latent:
        ns_ref[0, 1:] = jnp.zeros((DEPTH - 1, 2, SSD_D_INNER, SSD_D_STATE), F32)

    s_idx = lax.broadcasted_iota(jnp.int32, (SSD_CHUNK, SSD_CHUNK), 0)
    l_idx = lax.broadcasted_iota(jnp.int32, (SSD_CHUNK, SSD_CHUNK), 1)
    visible = (s_idx <= l_idx, s_idx >= l_idx)

    def outputs(c, carry):
        rows = rows_of(c)
        acs_t = [acst_ref[d, :, rows] for d in range(2)]
        e_acs_t = [jnp.exp(a) for a in acs_t]
        dt_t = [dtt_ref[dir_rows(d), rows] for d in range(2)]
        for g in range(SSD_GROUPS):
            bg = bcs_ref[rows, b_cols(g)]
            cg = bcs_ref[rows, c_cols(g)]
            g_t = lax.dot_general(bg, cg, _NT, preferred_element_type=F32)
            y_in = [lax.dot_general(st_ref[d, c, g * GROUP_ROWS:(g + 1) * GROUP_ROWS, :].astype(BF16), cg, _NT,
                                    preferred_element_type=F32) for d in range(2)]
            for hg in range(HEADS_PER_GROUP):
                h = g * HEADS_PER_GROUP + hg
                x_h = xt_ref[head_rows(h), rows]
                lhs, rhs = [], []
                y_h = dsk_ref[0, head_rows(h), :] * x_h
                for d in range(2):
                    seg = jnp.where(visible[d], acs_t[d][h:h + 1, :] - acs_ref[d, rows, h:h + 1], -jnp.inf)
                    rhs.append((g_t * jnp.exp(seg)).astype(BF16))
                    lhs.append((x_h * dt_t[d][h:h + 1, :]).astype(BF16))
                    y_h = y_h + y_in[d][hg * SSD_HEAD_DIM:(hg + 1) * SSD_HEAD_DIM, :] * e_acs_t[d][h:h + 1, :]
                y_h = y_h + jnp.dot(jnp.concatenate(lhs, axis=1), jnp.concatenate(rhs, axis=0),
                                    preferred_element_type=F32)
                yt_ref[head_rows(h), :] = y_h
        y_ref[rows, :] = yt_ref[...].T
        return carry

    lax.fori_loop(0, nc, outputs, 0)


def _scan(xbc, dt, dt_t, tri, alr, alc, dskip, scw, scb, state_ssd5, ns_all, l, latent):
    seq = DEC_SEQ if latent else SEQ
    n_seq = DEC_BATCH if latent else BATCH
    first = T_CTX // seq if latent else 0
    nc = seq // SSD_CHUNK
    first_layer = ns_all is None
    in_specs = [
        pl.BlockSpec((seq, SSD_XBC), lambda i: (first + i, 0), pipeline_mode=pl.Buffered(1 if latent else 2)),
        pl.BlockSpec((seq, 2 * DT_LANES), lambda i: (first + i, 0)),
        pl.BlockSpec((2 * SSD_HEADS, seq), lambda i: (0, first + i)),
        pl.BlockSpec((2, SSD_CHUNK, SSD_CHUNK), lambda i: (0, 0, 0), pipeline_mode=pl.Buffered(1)),
        _layer_block(l, (2, 1, DT_LANES)),
        _layer_block(l, (2, SSD_HEADS, 1)),
        _layer_block(l, (SSD_D_INNER, SSD_D_STATE)),
        _layer_block(l, (SSD_CONV, SSD_XBC)),
        _layer_block(l, (1, SSD_XBC)),
    ]
    args = (xbc, dt, dt_t, tri, alr, alc, dskip, scw, scb)
    scratch = [pltpu.VMEM((seq + 2 * CONV_PAD, SSD_XBC), F32),
               pltpu.VMEM((SSD_D_INNER, seq), F32),
               pltpu.VMEM((seq, BC_COLS), BF16),
               pltpu.VMEM((2, seq, DT_LANES), F32),
               pltpu.VMEM((2, SSD_HEADS, seq), F32),
               pltpu.VMEM((2, nc, SSD_D_INNER, SSD_D_STATE), F32),
               pltpu.VMEM((SSD_D_INNER, SSD_D_STATE), F32),
               pltpu.VMEM((SSD_D_INNER, SSD_CHUNK), F32)]
    kern = functools.partial(_scan_kernel, nc=nc, latent=latent, first_layer=first_layer)
    y_spec = pl.BlockSpec((seq, SSD_D_INNER), lambda i: (i, 0))
    if latent:
        state_block = (1, 1, 2, SSD_D_INNER, SSD_D_STATE)
        return pl.pallas_call(
            kern, grid=(n_seq,),
            in_specs=in_specs + [pl.BlockSpec(state_block, lambda i: (i, l, 0, 0, 0))],
            out_specs=y_spec,
            out_shape=jax.ShapeDtypeStruct((T_LAT, SSD_D_INNER), F32),
            scratch_shapes=scratch,
            compiler_params=_cparams(("arbitrary",)),
            name="ssd_scan_latent",
        )(*args, state_ssd5)
    ns_shape = jax.ShapeDtypeStruct((BATCH, DEPTH, 2, SSD_D_INNER, SSD_D_STATE), F32)
    y_shape = jax.ShapeDtypeStruct((T_CTX, SSD_D_INNER), F32)
    if first_layer:
        return pl.pallas_call(
            kern, grid=(n_seq,),
            in_specs=in_specs,
            out_specs=[y_spec, pl.BlockSpec((1, DEPTH, 2, SSD_D_INNER, SSD_D_STATE), lambda i: (i, 0, 0, 0, 0))],
            out_shape=[y_shape, ns_shape],
            scratch_shapes=scratch,
            compiler_params=_cparams(("arbitrary",)),
            name="ssd_scan_ctx",
        )(*args)
    return pl.pallas_call(
        kern, grid=(n_seq,),
        in_specs=in_specs + [pl.BlockSpec(memory_space=pl.ANY)],
        out_specs=[y_spec, pl.BlockSpec((1, 1, 2, SSD_D_INNER, SSD_D_STATE), lambda i: (i, l, 0, 0, 0))],
        out_shape=[y_shape, ns_shape],
        scratch_shapes=scratch,
        input_output_aliases={len(args): 1},
        compiler_params=_cparams(("arbitrary",)),
        name="ssd_scan_ctx",
    )(*args, ns_all)


CONF_SHIFT_ROWS = TILE + 3 * 8


def _fill_padded(pad_ref, cur_ref, prev_ref, next_ref, has_prev, has_next):
    pad_ref[0:HALO, :] = jnp.where(has_prev, prev_ref[...], 0.0)
    pad_ref[HALO:HALO + TILE, :] = cur_ref[...]
    pad_ref[HALO + TILE:, :] = jnp.where(has_next, next_ref[...], 0.0)


def _merge_kernel(x_ref, mod_ref, yc_ref, yl_ref, z_ref, gate_ref,
                  vc_ref, vp_ref, vn_ref, pc_ref, pp_ref, pn_ref,
                  ccw_ref, ccb_ref, lng_ref, lnb_ref, wbc_ref, pw_ref, psc_ref, wbp_ref,
                  sng_ref, wbs_ref, wout_ref, gpost_ref,
                  o_ref,
                  vpad_ref, vsh_ref, ppad_ref, cv_ref, mix_ref):
    i = pl.program_id(0)
    lat = i >= CTX_TILES
    k = jnp.maximum(i - CTX_TILES, 0) % LAT_TILES
    has_prev = jnp.logical_and(lat, k != 0)
    has_next = jnp.logical_and(lat, k != LAT_TILES - 1)
    _fill_padded(vpad_ref, vc_ref, vp_ref, vn_ref, has_prev, has_next)
    _fill_padded(ppad_ref, pc_ref, pp_ref, pn_ref, has_prev, has_next)

    for s in range(1, 8):
        vsh_ref[s - 1] = vpad_ref[s:s + CONF_SHIFT_ROWS, :]

    for cb in range(CONF_D // 128):
        col = slice(cb * 128, (cb + 1) * 128)
        first = HALO - CONF_KERNEL // 2
        acc = ccb_ref[0, :, col]
        for t in range(CONF_KERNEL):
            off = first + t
            a, s = off // 8, off % 8
            src = vpad_ref if s == 0 else vsh_ref.at[s - 1]
            acc = acc + ccw_ref[0, t:t + 1, col] * src[8 * a:8 * a + TILE, col]
        cv_ref[:, col] = acc
    cv = cv_ref[...]
    mu = jnp.mean(cv, axis=-1, keepdims=True)
    cen = cv - mu
    var = jnp.mean(cen * cen, axis=-1, keepdims=True)
    ln = cen * lax.rsqrt(var + EPS) * lng_ref[0] + lnb_ref[0]
    br_conf = jnp.dot(_silu(ln).astype(BF16), wbc_ref[0], preferred_element_type=F32)

    seq_len = jnp.where(lat, DEC_SEQ, SEQ)
    pos = k * TILE + lax.broadcasted_iota(jnp.int32, (TILE, 1), 0)
    for gi, w in enumerate(POOL_WINDOWS):
        cols = slice(gi * 128, (gi + 1) * 128)
        first = HALO - w // 2
        s = ppad_ref[first:first + TILE, cols]
        for j in range(1, w):
            s = s + ppad_ref[first + j:first + j + TILE, cols]
        lo = jnp.maximum(pos - w // 2, 0)
        hi = jnp.minimum(pos - w // 2 + w, seq_len)
        pooled = s / (hi - lo).astype(F32) - pc_ref[:, cols]
        mixed = jnp.dot(pooled.astype(BF16), pw_ref[0, gi], preferred_element_type=F32)
        mix_ref[:, cols] = mixed * psc_ref[0, :, cols]
    br_pool = jnp.dot(mix_ref[...].astype(BF16), wbp_ref[0], preferred_element_type=F32)

    y = jnp.where(lat, yl_ref[...], yc_ref[...])
    br_ssd = jnp.dot(_rms(y * z_ref[...], sng_ref[0]).astype(BF16), wbs_ref[0], preferred_element_type=F32)

    merged = (gate_ref[:, 0:D_MODEL] * br_ssd
              + gate_ref[:, D_MODEL:2 * D_MODEL] * br_conf
              + gate_ref[:, 2 * D_MODEL:] * br_pool)
    yo = jnp.dot(merged.astype(BF16), wout_ref[0], preferred_element_type=F32)
    gt = _mod_vec(mod_ref, _mod_row(i, TILE), 5)
    o_ref[...] = x_ref[...] + gt * _rms(yo, gpost_ref[0, 0])


def _merge(x, mod, y_ctx, y_lat, z, gate, v, pool, ccw, ccb, lng, lnb, wbc, pw, psc, wbp, sng, wbs, wout,
           norm_g, l):
    per_tile = TILE // HALO
    n_halo = T_ALL // HALO

    def cur(n):
        return pl.BlockSpec((TILE, n), lambda i: (i, 0))

    def trio(n):
        return [cur(n),
                pl.BlockSpec((HALO, n), lambda i: (jnp.maximum(i * per_tile - 1, 0), 0)),
                pl.BlockSpec((HALO, n), lambda i: (jnp.minimum((i + 1) * per_tile, n_halo - 1), 0))]

    return pl.pallas_call(
        _merge_kernel,
        grid=(N_TILES,),
        in_specs=[
            cur(D_MODEL),
            _layer_block(l, (8, N_MOD * D_MODEL)),
            pl.BlockSpec((TILE, SSD_D_INNER), lambda i: (jnp.minimum(i, CTX_TILES - 1), 0)),
            pl.BlockSpec((TILE, SSD_D_INNER), lambda i: (jnp.maximum(i - CTX_TILES, 0), 0)),
            cur(SSD_D_INNER), cur(3 * D_MODEL)]
        + trio(CONF_D) + trio(POOL_D) + [
            _layer_block(l, (CONF_KERNEL, CONF_D)), _layer_block(l, (1, CONF_D)),
            _layer_block(l, (1, CONF_D)), _layer_block(l, (1, CONF_D)),
            _layer_block(l, (CONF_D, D_MODEL)),
            _layer_block(l, (len(POOL_WINDOWS), 128, 128)), _layer_block(l, (1, POOL_D)),
            _layer_block(l, (POOL_D, D_MODEL)),
            _layer_block(l, (1, SSD_D_INNER)),
            _layer_block(l, (SSD_D_INNER, D_MODEL)),
            _layer_block(l, (D_MODEL, D_MODEL)),
            pl.BlockSpec((1, 1, 1, D_MODEL), lambda i: (l, 3, 0, 0), pipeline_mode=pl.Buffered(1)),
        ],
        out_specs=cur(D_MODEL),
        out_shape=jax.ShapeDtypeStruct((T_ALL, D_MODEL), F32),
        scratch_shapes=[pltpu.VMEM((TILE + 2 * HALO, CONF_D), F32),
                        pltpu.VMEM((7, CONF_SHIFT_ROWS, CONF_D), F32),
                        pltpu.VMEM((TILE + 2 * HALO, POOL_D), F32),
                        pltpu.VMEM((TILE, CONF_D), F32),
                        pltpu.VMEM((TILE, POOL_D), F32)],
        compiler_params=_cparams(("arbitrary",)),
        name="mixer_merge",
    )(x, mod, y_ctx, y_lat, z, gate, v, v, v, pool, pool, pool,
      ccw, ccb, lng, lnb, wbc, pw, psc, wbp, sng, wbs, wout, norm_g)


def _pos_embed_2d(n_tokens):
    rows = n_tokens // GRID_W
    r, col = jnp.meshgrid(jnp.arange(rows), jnp.arange(GRID_W), indexing='ij')
    r = r.reshape(-1).astype(F32)
    col = col.reshape(-1).astype(F32)
    q = D_MODEL // 4
    omega = 1.0 / (10000.0 ** (jnp.arange(q, dtype=F32) / q))
    ar = r[:, None] * omega
    ac = col[:, None] * omega
    return jnp.concatenate([jnp.sin(ar), jnp.cos(ar), jnp.sin(ac), jnp.cos(ac)], axis=-1)


def _pad_lanes(a, n):
    return jnp.pad(a, [(0, 0)] * (a.ndim - 1) + [(0, n - a.shape[-1])])


def kernel(x_prompt, x_sample, state_ssd, c, c_ctx, w_mod, b_mod, norm_g, w_ffn_in, w_ffn_out, w_in,
           ssd_conv_w, ssd_conv_b, ssd_a_log, ssd_dt_bias, ssd_d, ssd_norm_g, w_br_ssd, conf_conv_w,
           conf_conv_b, conf_ln_g, conf_ln_b, w_br_conf, pool_w, pool_scale, w_br_pool, w_out):
    xs = x_sample + _pos_embed_2d(DEC_SEQ).astype(x_sample.dtype)[None]
    x = (x_prompt.reshape(T_CTX, D_MODEL), xs.reshape(T_LAT, D_MODEL))

    cond_t = jnp.concatenate([c_ctx[None, :], c, jnp.zeros((8 - 1 - DEC_BATCH, D_MODEL), F32)], axis=0).T
    mod = _modulation(cond_t, w_mod, b_mod)

    wi_b = w_ffn_in.astype(BF16)
    wo_b = w_ffn_out.astype(BF16)
    wa_b = w_in[:, :, :OFF_XBC].astype(BF16)
    wb_b = w_in[:, :, OFF_DT:].astype(BF16)
    w_dt = w_in[:, :, OFF_XBC:OFF_DT]
    wdt_b = jnp.concatenate([_pad_lanes(w_dt[:, :, :SSD_HEADS], DT_LANES),
                             _pad_lanes(w_dt[:, :, SSD_HEADS:], DT_LANES)], axis=-1).astype(BF16)
    wdt_t_b = jnp.swapaxes(w_dt, 1, 2).astype(BF16)
    dtb_row = _pad_lanes(ssd_dt_bias, DT_LANES).reshape(DEPTH, 1, 2 * DT_LANES)
    dtb_col = ssd_dt_bias.reshape(DEPTH, 2 * SSD_HEADS, 1)
    wbs_b = w_br_ssd.astype(BF16)
    wbc_b = w_br_conf.astype(BF16)
    wbp_b = w_br_pool.astype(BF16)
    wout_b = w_out.astype(BF16)
    pw_b = pool_w.astype(BF16)
    ng = norm_g.reshape(DEPTH, 6, 1, D_MODEL)
    alr = _pad_lanes(ssd_a_log, DT_LANES).reshape(DEPTH, 2, 1, DT_LANES)
    alc = ssd_a_log.reshape(DEPTH, 2, SSD_HEADS, 1)
    dskip = jnp.broadcast_to(jnp.repeat(ssd_d, SSD_HEAD_DIM, axis=1)[:, :, None],
                             (DEPTH, SSD_D_INNER, SSD_D_STATE))
    idx = jnp.arange(SSD_CHUNK)
    tri = jnp.stack([idx[:, None] >= idx[None, :], idx[:, None] <= idx[None, :]]).astype(BF16)
    state5 = state_ssd.reshape(DEC_BATCH, DEPTH, 2, SSD_D_INNER, SSD_D_STATE)
    ns_all = None

    def row1(a):
        return a.reshape(DEPTH, 1, a.shape[-1])

    for l in range(DEPTH):
        x = _ffn(x, mod, ng, wi_b, wo_b, l, 0)
        z, xbc, v, pool, gate, dt, dt_t = _proj(x, mod, ng, wa_b, wb_b, wdt_b, wdt_t_b, dtb_row, dtb_col, l)
        scan_args = (xbc, dt, dt_t, tri, alr, alc, dskip, ssd_conv_w, row1(ssd_conv_b))
        y_ctx, ns_all = _scan(*scan_args, None, ns_all, l, latent=False)
        y_lat = _scan(*scan_args, state5, None, l, latent=True)
        x = _merge(x, mod, y_ctx, y_lat, z, gate, v, pool, conf_conv_w, row1(conf_conv_b),
                   row1(conf_ln_g), row1(conf_ln_b), wbc_b, pw_b, row1(pool_scale), wbp_b,
                   row1(ssd_norm_g), wbs_b, wout_b, ng, l)
        x = _ffn(x, mod, ng, wi_b, wo_b, l, 1, split_out=(l == DEPTH - 1))

    y_prompt = x[0].reshape(BATCH, SEQ, D_MODEL)
    y_sample = x[1].reshape(DEC_BATCH, DEC_SEQ, D_MODEL)
    new_state = ns_all.reshape(BATCH, DEPTH, 2, SSD_HEADS, SSD_HEAD_DIM, SSD_D_STATE).astype(x_prompt.dtype)
    return (y_prompt, y_sample, new_state)
```

```python
import functools

import jax
import jax.numpy as jnp
from jax import lax
from jax.experimental import pallas as pl
from jax.experimental.pallas import tpu as pltpu

F32 = jnp.float32
BF16 = jnp.bfloat16

D_MODEL = 1024
BATCH = 32
SEQ = 256
DEPTH = 4
DEC_BATCH = 2
DEC_SEQ = 1024
GRID_W = 64
SSD_D_INNER = 1024
SSD_HEAD_DIM = 64
SSD_HEADS = 16
SSD_GROUPS = 4
SSD_D_STATE = 128
SSD_CONV = 5
SSD_CHUNK = 128
SSD_XBC = 2048
CONF_D = 512
CONF_KERNEL = 31
POOL_D = 512
POOL_WINDOWS = (2, 4, 8, 16)
D_FF = 2816
N_MOD = 9
FFN_RES = 0.5
EPS = 1e-6
OFF_XBC = 3072
OFF_DT = 3104
IN_COLS = 7712

T_CTX = BATCH * SEQ
T_LAT = DEC_BATCH * DEC_SEQ
T_ALL = T_CTX + T_LAT
TILE = 256
HALO = 16
N_TILES = T_ALL // TILE
CTX_TILES = T_CTX // TILE
LAT_TILES = DEC_SEQ // TILE
HEADS_PER_GROUP = SSD_HEADS // SSD_GROUPS
GROUP_ROWS = HEADS_PER_GROUP * SSD_HEAD_DIM
BC_COLS = 2 * SSD_GROUPS * SSD_D_STATE
CONV_PAD = 8

FF_CHUNK = 256
N_FF_CHUNKS = D_FF // FF_CHUNK
TM_FFN = 512

N_WA = OFF_XBC
N_WB = IN_COLS - OFF_DT
PB_POOL = 2 * CONF_D
PB_GATE = PB_POOL + POOL_D
DT_LANES = 128

VMEM_LIMIT = 56 * 1024 * 1024


def _cparams(sem):
    return pltpu.CompilerParams(dimension_semantics=sem, vmem_limit_bytes=VMEM_LIMIT)


def _layer_block(l, shape):
    nd = len(shape)
    return pl.BlockSpec((1,) + tuple(shape), lambda *_: (l,) + (0,) * nd, pipeline_mode=pl.Buffered(1))


def _rms(x, g):
    ms = jnp.mean(x * x, axis=-1, keepdims=True)
    return x * lax.rsqrt(ms + EPS) * g


def _sigmoid(x):
    return 0.5 * jnp.tanh(0.5 * x) + 0.5


def _silu(x):
    u = 0.5 * x
    return u * jnp.tanh(u) + u


def _softplus(x):
    return jnp.maximum(x, 0.0) + jnp.log(1.0 + jnp.exp(-jnp.abs(x)))


def _mod_row(i, tm):
    ctx_tiles = T_CTX // tm
    per_seq = DEC_SEQ // tm
    return jnp.where(i < ctx_tiles, 0, 1 + jnp.maximum(i - ctx_tiles, 0) // per_seq)


def _mod_vec(mod_ref, row, k):
    return mod_ref[0, pl.ds(row, 1), pl.ds(k * D_MODEL, D_MODEL)]


def _mod_kernel(ct_ref, w_ref, b_ref, o_ref):
    ct = ct_ref[...]
    s = _silu(ct)
    w = w_ref[0]
    b = b_ref[0]
    o_ref[0] = jnp.zeros(o_ref.shape[1:], F32)
    for r in range(1 + DEC_BATCH):
        o_ref[0, r:r + 1, :] = jnp.sum(s[:, r:r + 1] * w, axis=0, keepdims=True) + b


def _modulation(cond_t, w_mod, b_mod):
    tn = 1024
    n_cols = N_MOD * D_MODEL
    return pl.pallas_call(
        _mod_kernel,
        grid=(DEPTH, n_cols // tn),
        in_specs=[
            pl.BlockSpec((D_MODEL, 8), lambda l, j: (0, 0)),
            pl.BlockSpec((1, D_MODEL, tn), lambda l, j: (l, 0, j)),
            pl.BlockSpec((1, 1, tn), lambda l, j: (l, 0, j)),
        ],
        out_specs=pl.BlockSpec((1, 8, tn), lambda l, j: (l, 0, j)),
        out_shape=jax.ShapeDtypeStruct((DEPTH, 8, n_cols), F32),
        compiler_params=_cparams(("arbitrary", "arbitrary")),
        name="modulation",
    )(cond_t, w_mod, b_mod.reshape(DEPTH, 1, n_cols))


def _ffn_kernel(*refs, tm, k0, l, f, split_in, split_out):
    refs = list(refs)
    x_refs = [refs.pop(0) for _ in range(2 if split_in else 1)]
    mod_ref, gpre_ref, gpost_ref, wi_hbm, wo_hbm = refs[:5]
    n_out = 2 if split_out else 1
    o_refs = refs[5:5 + n_out]
    acc_ref, wi_ref, wo_ref, sg_ref, su_ref, so_ref, sem = refs[5 + n_out:]
    i = pl.program_id(0)
    is_ctx = i < T_CTX // tm
    row = _mod_row(i, tm)
    x = jnp.where(is_ctx, x_refs[0][...], x_refs[1][...]) if split_in else x_refs[0][...]
    sh = _mod_vec(mod_ref, row, k0)
    sc = _mod_vec(mod_ref, row, k0 + 1)
    gt = _mod_vec(mod_ref, row, k0 + 2)
    h = (_rms(x, gpre_ref[0, 0]) * (1.0 + sc) + sh).astype(BF16)

    def chunk_copies(c):
        lo, slot = c * FF_CHUNK, c % 2
        return (pltpu.make_async_copy(wi_hbm.at[l, f, :, pl.ds(lo, FF_CHUNK)], sg_ref.at[slot], sem.at[0, slot]),
                pltpu.make_async_copy(wi_hbm.at[l, f, :, pl.ds(D_FF + lo, FF_CHUNK)], su_ref.at[slot],
                                      sem.at[1, slot]),
                pltpu.make_async_copy(wo_hbm.at[l, f, pl.ds(lo, FF_CHUNK), :], so_ref.at[slot], sem.at[2, slot]))

    def run(stream_weights):
        if stream_weights:
            for cp in chunk_copies(0):
                cp.start()
        for c in range(N_FF_CHUNKS):
            lo = c * FF_CHUNK
            if stream_weights:
                if c + 1 < N_FF_CHUNKS:
                    for cp in chunk_copies(c + 1):
                        cp.start()
                for cp in chunk_copies(c):
                    cp.wait()
                slot = c % 2
                wi_ref[:, lo:lo + FF_CHUNK] = sg_ref[slot].astype(BF16)
                wi_ref[:, D_FF + lo:D_FF + lo + FF_CHUNK] = su_ref[slot].astype(BF16)
                wo_ref[lo:lo + FF_CHUNK, :] = so_ref[slot].astype(BF16)
            g = jnp.dot(h, wi_ref[:, lo:lo + FF_CHUNK], preferred_element_type=F32)
            u = jnp.dot(h, wi_ref[:, D_FF + lo:D_FF + lo + FF_CHUNK], preferred_element_type=F32)
            a = (_silu(g) * u).astype(BF16)
            part = jnp.dot(a, wo_ref[lo:lo + FF_CHUNK, :], preferred_element_type=F32)
            if c == 0:
                acc_ref[...] = part
            else:
                acc_ref[...] += part
        out = x + (FFN_RES * gt) * _rms(acc_ref[...], gpost_ref[0, 0])
        if split_out:
            @pl.when(is_ctx)
            def _():
                o_refs[0][...] = out

            @pl.when(jnp.logical_not(is_ctx))
            def _():
                o_refs[1][...] = out
        else:
            o_refs[0][...] = out

    pl.when(i == 0)(functools.partial(run, True))
    pl.when(i != 0)(functools.partial(run, False))


def _ffn(xs, mod, norm_g, w_ffn_in, w_ffn_out, l, f, split_out=False):
    tm = TM_FFN
    ctx_tiles = T_CTX // tm
    split_in = isinstance(xs, tuple)

    def lf_block(shape, k):
        return pl.BlockSpec((1, 1) + shape, lambda i: (l, k, 0, 0), pipeline_mode=pl.Buffered(1))

    merged = pl.BlockSpec((tm, D_MODEL), lambda i: (i, 0))
    ctx_part = pl.BlockSpec((tm, D_MODEL), lambda i: (jnp.minimum(i, ctx_tiles - 1), 0))
    lat_part = pl.BlockSpec((tm, D_MODEL), lambda i: (jnp.maximum(i - ctx_tiles, 0), 0))
    parts_shape = [jax.ShapeDtypeStruct((T_CTX, D_MODEL), F32), jax.ShapeDtypeStruct((T_LAT, D_MODEL), F32)]
    return pl.pallas_call(
        functools.partial(_ffn_kernel, tm=tm, k0=6 * f, l=l, f=f, split_in=split_in, split_out=split_out),
        grid=(T_ALL // tm,),
        in_specs=([ctx_part, lat_part] if split_in else [merged]) + [
            _layer_block(l, (8, N_MOD * D_MODEL)),
            lf_block((1, D_MODEL), 4 * f),
            lf_block((1, D_MODEL), 4 * f + 1),
            pl.BlockSpec(memory_space=pl.ANY),
            pl.BlockSpec(memory_space=pl.ANY),
        ],
        out_specs=[ctx_part, lat_part] if split_out else merged,
        out_shape=parts_shape if split_out else jax.ShapeDtypeStruct((T_ALL, D_MODEL), F32),
        scratch_shapes=[pltpu.VMEM((tm, D_MODEL), F32),
                        pltpu.VMEM((D_MODEL, 2 * D_FF), BF16),
                        pltpu.VMEM((D_FF, D_MODEL), BF16),
                        pltpu.VMEM((2, D_MODEL, FF_CHUNK), F32),
                        pltpu.VMEM((2, D_MODEL, FF_CHUNK), F32),
                        pltpu.VMEM((2, FF_CHUNK, D_MODEL), F32),
                        pltpu.SemaphoreType.DMA((3, 2))],
        compiler_params=_cparams(("arbitrary",)),
        name="ffn",
    )(*(xs if split_in else (xs,)), mod, norm_g, norm_g, w_ffn_in, w_ffn_out)


def _proj_kernel(x_ref, mod_ref, g_ref, w_ref, wdt_ref, wdt_t_ref, dtb_row_ref, dtb_col_ref,
                 z_ref, xbc_ref, v_ref, pool_ref, gate_ref, dt_ref, dt_t_ref, wb_ref):
    i = pl.program_id(0)
    row = _mod_row(i, TILE)
    sh = _mod_vec(mod_ref, row, 3)
    sc = _mod_vec(mod_ref, row, 4)
    h = (_rms(x_ref[...], g_ref[0, 0]) * (1.0 + sc) + sh).astype(BF16)

    @pl.when(i == 0)
    def _():
        wb_ref[...] = w_ref[0, :, OFF_DT:IN_COLS]

    def mm(w, lo, hi):
        return jnp.dot(h, w[:, lo:hi], preferred_element_type=F32)

    wa = w_ref.at[0]
    z_ref[...] = _silu(mm(wa, 0, SSD_D_INNER))
    xbc_ref[...] = mm(wa, SSD_D_INNER, N_WA)
    ag = mm(wb_ref, 0, PB_POOL)
    v_ref[...] = ag[:, :CONF_D] * _sigmoid(ag[:, CONF_D:])
    pool_ref[...] = mm(wb_ref, PB_POOL, PB_GATE)
    gate_ref[...] = _sigmoid(mm(wb_ref, PB_GATE, N_WB))
    dt_ref[...] = _softplus(jnp.dot(h, wdt_ref[0], preferred_element_type=F32) + dtb_row_ref[0])
    dt_t = lax.dot_general(wdt_t_ref[0], h, (((1,), (1,)), ((), ())), preferred_element_type=F32)
    dt_t_ref[...] = _softplus(dt_t + dtb_col_ref[0])


def _proj(x, mod, norm_g, w_b, wdt, wdt_t, dtb_row, dtb_col, l):
    def tile(n):
        return pl.BlockSpec((TILE, n), lambda i: (i, 0))

    widths = (SSD_D_INNER, SSD_XBC, CONF_D, POOL_D, 3 * D_MODEL, 2 * DT_LANES)
    return pl.pallas_call(
        _proj_kernel,
        grid=(N_TILES,),
        in_specs=[
            tile(D_MODEL),
            _layer_block(l, (8, N_MOD * D_MODEL)),
            pl.BlockSpec((1, 1, 1, D_MODEL), lambda i: (l, 2, 0, 0), pipeline_mode=pl.Buffered(1)),
            _layer_block(l, (D_MODEL, IN_COLS)),
            _layer_block(l, (D_MODEL, 2 * DT_LANES)),
            _layer_block(l, (2 * SSD_HEADS, D_MODEL)),
            _layer_block(l, (1, 2 * DT_LANES)),
            _layer_block(l, (2 * SSD_HEADS, 1)),
        ],
        out_specs=[tile(n) for n in widths] + [pl.BlockSpec((2 * SSD_HEADS, TILE), lambda i: (0, i))],
        out_shape=[jax.ShapeDtypeStruct((T_ALL, n), F32) for n in widths]
        + [jax.ShapeDtypeStruct((2 * SSD_HEADS, T_ALL), F32)],
        scratch_shapes=[pltpu.VMEM((D_MODEL, N_WB), BF16)],
        compiler_params=_cparams(("arbitrary",)),
        name="in_proj",
    )(x, mod, norm_g, w_b, wdt, wdt_t, dtb_row, dtb_col)


_NT = (((1,), (1,)), ((), ()))


def _split3(v):
    p0 = v.astype(BF16)
    r = v - p0.astype(F32)
    p1 = r.astype(BF16)
    p2 = (r - p1.astype(F32)).astype(BF16)
    return p0, p1, p2


def _scan_kernel(*refs, nc, latent, first_layer):
    (xbc_ref, dt_ref, dtt_ref, tri_ref, alr_ref, alc_ref, dsk_ref, scw_ref, scb_ref) = refs[:9]
    refs = refs[9:]
    if latent:
        h0_ref, y_ref = refs[:2]
        refs = refs[2:]
    elif first_layer:
        y_ref, ns_ref = refs[:2]
        refs = refs[2:]
    else:
        y_ref, ns_ref = refs[1:3]
        refs = refs[3:]
    xpad_ref, xt_ref, bcs_ref, acs_ref, acst_ref, st_ref, s_ref, yt_ref = refs
    seq = nc * SSD_CHUNK

    def rows_of(c):
        return pl.ds(pl.multiple_of(c * SSD_CHUNK, SSD_CHUNK), SSD_CHUNK)

    def dir_rows(d):
        return slice(d * SSD_HEADS, (d + 1) * SSD_HEADS)

    def b_cols(g):
        return slice(g * SSD_D_STATE, (g + 1) * SSD_D_STATE)

    def c_cols(g):
        lo = SSD_GROUPS * SSD_D_STATE + g * SSD_D_STATE
        return slice(lo, lo + SSD_D_STATE)

    def head_rows(h):
        return slice(h * SSD_HEAD_DIM, (h + 1) * SSD_HEAD_DIM)

    def total_col(acs_t, d):
        return acs_t[:, SSD_CHUNK - 1:] if d == 0 else acs_t[:, :1]

    zeros_pad = jnp.zeros((CONV_PAD, SSD_XBC), F32)
    xpad_ref[0:CONV_PAD, :] = zeros_pad
    xpad_ref[CONV_PAD + seq:, :] = zeros_pad
    xpad_ref[CONV_PAD:CONV_PAD + seq, :] = xbc_ref[...]
    for ct in range(nc):
        first = CONV_PAD + ct * SSD_CHUNK - SSD_CONV // 2

        def conv_block(cb, first=first):
            col = pl.ds(pl.multiple_of(cb * 128, 128), 128)
            acc = scb_ref[0, :, col]
            for t in range(SSD_CONV):
                acc = acc + scw_ref[0, pl.ds(t, 1), col] * xpad_ref[pl.ds(first + t, SSD_CHUNK), col]
            return _silu(acc)

        def conv_x(cb, carry, ct=ct, conv_block=conv_block):
            xt_ref[pl.ds(pl.multiple_of(cb * 128, 128), 128), ct * SSD_CHUNK:(ct + 1) * SSD_CHUNK] = conv_block(cb).T
            return carry

        def conv_bc(cb, carry, ct=ct, conv_block=conv_block):
            col = pl.ds(pl.multiple_of(cb * 128, 128), 128)
            bcs_ref[ct * SSD_CHUNK:(ct + 1) * SSD_CHUNK, col] = conv_block(cb + SSD_D_INNER // 128).astype(BF16)
            return carry

        lax.fori_loop(0, SSD_D_INNER // 128, conv_x, 0, unroll=2)
        lax.fori_loop(0, BC_COLS // 128, conv_bc, 0)

    def cumsums(c, carry):
        rows = rows_of(c)
        for d in range(2):
            tri = tri_ref[d]
            a_row = -jnp.exp(alr_ref[0, d])
            a_col = -jnp.exp(alc_ref[0, d])
            p = jnp.concatenate(_split3(dt_ref[rows, d * DT_LANES:(d + 1) * DT_LANES]), axis=1)
            r = jnp.dot(tri, p, preferred_element_type=F32)
            acs_ref[d, rows, :] = (r[:, :128] + r[:, 128:256] + r[:, 256:]) * a_row
            q = jnp.concatenate(_split3(dtt_ref[dir_rows(d), rows]), axis=0)
            rt = lax.dot_general(q, tri, _NT, preferred_element_type=F32)
            acst_ref[d, :, rows] = (rt[:SSD_HEADS] + rt[SSD_HEADS:2 * SSD_HEADS] + rt[2 * SSD_HEADS:]) * a_col
        return carry

    lax.fori_loop(0, nc, cumsums, 0)

    def local_states(c, carry):
        rows = rows_of(c)
        xt = xt_ref[:, rows]
        for d in range(2):
            acs_t = acst_ref[d, :, rows]
            w = dtt_ref[dir_rows(d), rows] * jnp.exp(total_col(acs_t, d) - acs_t)
            for g in range(SSD_GROUPS):
                bg = bcs_ref[rows, b_cols(g)]
                parts = [(xt[head_rows(h), :] * w[h:h + 1, :]).astype(BF16)
                         for h in range(g * HEADS_PER_GROUP, (g + 1) * HEADS_PER_GROUP)]
                st_ref[d, c, g * GROUP_ROWS:(g + 1) * GROUP_ROWS, :] = jnp.dot(
                    jnp.concatenate(parts, axis=0), bg, preferred_element_type=F32)
        return carry

    lax.fori_loop(0, nc, local_states, 0)

    for d in range(2):
        if latent:
            s_ref[...] = h0_ref[0, 0, d]
        else:
            s_ref[...] = jnp.zeros(s_ref.shape, F32)

        def recur(j, carry, d=d):
            c = j if d == 0 else nc - 1 - j
            acs_t = acst_ref[d, :, rows_of(c)]
            e_tot = jnp.broadcast_to(jnp.exp(total_col(acs_t, d)), (SSD_HEADS, SSD_D_STATE))
            for h in range(SSD_HEADS):
                s_old = s_ref[head_rows(h), :]
                cs = st_ref[d, c, head_rows(h), :]
                st_ref[d, c, head_rows(h), :] = s_old
                s_ref[head_rows(h), :] = s_old * e_tot[h:h + 1, :] + cs
            return carry

        lax.fori_loop(0, nc, recur, 0)
        if not latent:
            ns_ref[0, 0, d] = s_ref[...]
    if first_layer and not latent:
        ns_ref[0, 1:] = jnp.zeros((DEPTH - 1, 2, SSD_D_INNER, SSD_D_STATE), F32)

    s_idx = lax.broadcasted_iota(jnp.int32, (SSD_CHUNK, SSD_CHUNK), 0)
    l_idx = lax.broadcasted_iota(jnp.int32, (SSD_CHUNK, SSD_CHUNK), 1)
    visible = (s_idx <= l_idx, s_idx >= l_idx)

    def outputs(c, carry):
        rows = rows_of(c)
        acs_t = [acst_ref[d, :, rows] for d in range(2)]
        e_acs_t = [jnp.exp(a) for a in acs_t]
        dt_t = [dtt_ref[dir_rows(d), rows] for d in range(2)]
        for g in range(SSD_GROUPS):
            bg = bcs_ref[rows, b_cols(g)]
            cg = bcs_ref[rows, c_cols(g)]
            g_t = lax.dot_general(bg, cg, _NT, preferred_element_type=F32)
            y_in = [lax.dot_general(st_ref[d, c, g * GROUP_ROWS:(g + 1) * GROUP_ROWS, :].astype(BF16), cg, _NT,
                                    preferred_element_type=F32) for d in range(2)]
            for hg in range(HEADS_PER_GROUP):
                h = g * HEADS_PER_GROUP + hg
                x_h = xt_ref[head_rows(h), rows]
                lhs, rhs = [], []
                y_h = dsk_ref[0, head_rows(h), :] * x_h
                for d in range(2):
                    seg = jnp.where(visible[d], acs_t[d][h:h + 1, :] - acs_ref[d, rows, h:h + 1], -jnp.inf)
                    rhs.append((g_t * jnp.exp(seg)).astype(BF16))
                    lhs.append((x_h * dt_t[d][h:h + 1, :]).astype(BF16))
                    y_h = y_h + y_in[d][hg * SSD_HEAD_DIM:(hg + 1) * SSD_HEAD_DIM, :] * e_acs_t[d][h:h + 1, :]
                y_h = y_h + jnp.dot(jnp.concatenate(lhs, axis=1), jnp.concatenate(rhs, axis=0),
                                    preferred_element_type=F32)
                yt_ref[head_rows(h), :] = y_h
        y_ref[rows, :] = yt_ref[...].T
        return carry

    lax.fori_loop(0, nc, outputs, 0)


def _scan(xbc, dt, dt_t, tri, alr, alc, dskip, scw, scb, state_ssd5, ns_all, l, latent):
    seq = DEC_SEQ if latent else SEQ
    n_seq = DEC_BATCH if latent else BATCH
    first = T_CTX // seq if latent else 0
    nc = seq // SSD_CHUNK
    first_layer = ns_all is None
    in_specs = [
        pl.BlockSpec((seq, SSD_XBC), lambda i: (first + i, 0), pipeline_mode=pl.Buffered(1 if latent else 2)),
        pl.BlockSpec((seq, 2 * DT_LANES), lambda i: (first + i, 0)),
        pl.BlockSpec((2 * SSD_HEADS, seq), lambda i: (0, first + i)),
        pl.BlockSpec((2, SSD_CHUNK, SSD_CHUNK), lambda i: (0, 0, 0), pipeline_mode=pl.Buffered(1)),
        _layer_block(l, (2, 1, DT_LANES)),
        _layer_block(l, (2, SSD_HEADS, 1)),
        _layer_block(l, (SSD_D_INNER, SSD_D_STATE)),
        _layer_block(l, (SSD_CONV, SSD_XBC)),
        _layer_block(l, (1, SSD_XBC)),
    ]
    args = (xbc, dt, dt_t, tri, alr, alc, dskip, scw, scb)
    scratch = [pltpu.VMEM((seq + 2 * CONV_PAD, SSD_XBC), F32),
               pltpu.VMEM((SSD_D_INNER, seq), F32),
               pltpu.VMEM((seq, BC_COLS), BF16),
               pltpu.VMEM((2, seq, DT_LANES), F32),
               pltpu.VMEM((2, SSD_HEADS, seq), F32),
               pltpu.VMEM((2, nc, SSD_D_INNER, SSD_D_STATE), F32),
               pltpu.VMEM((SSD_D_INNER, SSD_D_STATE), F32),
               pltpu.VMEM((SSD_D_INNER, SSD_CHUNK), F32)]
    kern = functools.partial(_scan_kernel, nc=nc, latent=latent, first_layer=first_layer)
    y_spec = pl.BlockSpec((seq, SSD_D_INNER), lambda i: (i, 0))
    if latent:
        state_block = (1, 1, 2, SSD_D_INNER, SSD_D_STATE)
        return pl.pallas_call(
            kern, grid=(n_seq,),
            in_specs=in_specs + [pl.BlockSpec(state_block, lambda i: (i, l, 0, 0, 0))],
            out_specs=y_spec,
            out_shape=jax.ShapeDtypeStruct((T_LAT, SSD_D_INNER), F32),
            scratch_shapes=scratch,
            compiler_params=_cparams(("arbitrary",)),
            name="ssd_scan_latent",
        )(*args, state_ssd5)
    ns_shape = jax.ShapeDtypeStruct((BATCH, DEPTH, 2, SSD_D_INNER, SSD_D_STATE), F32)
    y_shape = jax.ShapeDtypeStruct((T_CTX, SSD_D_INNER), F32)
    if first_layer:
        return pl.pallas_call(
            kern, grid=(n_seq,),
            in_specs=in_specs,
            out_specs=[y_spec, pl.BlockSpec((1, DEPTH, 2, SSD_D_INNER, SSD_D_STATE), lambda i: (i, 0, 0, 0, 0))],
            out_shape=[y_shape, ns_shape],
            scratch_shapes=scratch,
            compiler_params=_cparams(("arbitrary",)),
            name="ssd_scan_ctx",
        )(*args)
    return pl.pallas_call(
        kern, grid=(n_seq,),
        in_specs=in_specs + [pl.BlockSpec(memory_space=pl.ANY)],
        out_specs=[y_spec, pl.BlockSpec((1, 1, 2, SSD_D_INNER, SSD_D_STATE), lambda i: (i, l, 0, 0, 0))],
        out_shape=[y_shape, ns_shape],
        scratch_shapes=scratch,
        input_output_aliases={len(args): 1},
        compiler_params=_cparams(("arbitrary",)),
        name="ssd_scan_ctx",
    )(*args, ns_all)


CONF_SHIFT_ROWS = TILE + 3 * 8


def _fill_padded(pad_ref, cur_ref, prev_ref, next_ref, has_prev, has_next):
    pad_ref[0:HALO, :] = jnp.where(has_prev, prev_ref[...], 0.0)
    pad_ref[HALO:HALO + TILE, :] = cur_ref[...]
    pad_ref[HALO + TILE:, :] = jnp.where(has_next, next_ref[...], 0.0)


def _merge_kernel(x_ref, mod_ref, yc_ref, yl_ref, z_ref, gate_ref,
                  vc_ref, vp_ref, vn_ref, pc_ref, pp_ref, pn_ref,
                  ccw_ref, ccb_ref, lng_ref, lnb_ref, wbc_ref, pw_ref, psc_ref, wbp_ref,
                  sng_ref, wbs_ref, wout_ref, gpost_ref,
                  o_ref,
                  vpad_ref, vsh_ref, ppad_ref, cv_ref, mix_ref):
    i = pl.program_id(0)
    lat = i >= CTX_TILES
    k = jnp.maximum(i - CTX_TILES, 0) % LAT_TILES
    has_prev = jnp.logical_and(lat, k != 0)
    has_next = jnp.logical_and(lat, k != LAT_TILES - 1)
    _fill_padded(vpad_ref, vc_ref, vp_ref, vn_ref, has_prev, has_next)
    _fill_padded(ppad_ref, pc_ref, pp_ref, pn_ref, has_prev, has_next)

    y = jnp.where(lat, yl_ref[...], yc_ref[...])
    br_ssd = jnp.dot(_rms(y * z_ref[...], sng_ref[0]).astype(BF16), wbs_ref[0], preferred_element_type=F32)

    for s in range(1, 8):
        vsh_ref[s - 1] = vpad_ref[s:s + CONF_SHIFT_ROWS, :]

    for cb in range(CONF_D // 128):
        col = slice(cb * 128, (cb + 1) * 128)
        first = HALO - CONF_KERNEL // 2
        acc = ccb_ref[0, :, col]
        for t in range(CONF_KERNEL):
            off = first + t
            a, s = off // 8, off % 8
            src = vpad_ref if s == 0 else vsh_ref.at[s - 1]
            acc = acc + ccw_ref[0, t:t + 1, col] * src[8 * a:8 * a + TILE, col]
        cv_ref[:, col] = acc
    cv = cv_ref[...]
    mu = jnp.mean(cv, axis=-1, keepdims=True)
    cen = cv - mu
    var = jnp.mean(cen * cen, axis=-1, keepdims=True)
    ln = cen * lax.rsqrt(var + EPS) * lng_ref[0] + lnb_ref[0]
    br_conf = jnp.dot(_silu(ln).astype(BF16), wbc_ref[0], preferred_element_type=F32)

    seq_len = jnp.where(lat, DEC_SEQ, SEQ)
    pos = k * TILE + lax.broadcasted_iota(jnp.int32, (TILE, 1), 0)
    for gi, w in enumerate(POOL_WINDOWS):
        cols = slice(gi * 128, (gi + 1) * 128)
        first = HALO - w // 2
        s = ppad_ref[first:first + TILE, cols]
        for j in range(1, w):
            s = s + ppad_ref[first + j:first + j + TILE, cols]
        lo = jnp.maximum(pos - w // 2, 0)
        hi = jnp.minimum(pos - w // 2 + w, seq_len)
        pooled = s / (hi - lo).astype(F32) - pc_ref[:, cols]
        mixed = jnp.dot(pooled.astype(BF16), pw_ref[0, gi], preferred_element_type=F32)
        mix_ref[:, cols] = mixed * psc_ref[0, :, cols]
    br_pool = jnp.dot(mix_ref[...].astype(BF16), wbp_ref[0], preferred_element_type=F32)

    merged =(gate_ref[:, 0:D_MODEL] * br_ssd
              + gate_ref[:, D_MODEL:2 * D_MODEL] * br_conf
              + gate_ref[:, 2 * D_MODEL:] * br_pool)
    yo = jnp.dot(merged.astype(BF16), wout_ref[0], preferred_element_type=F32)
    gt = _mod_vec(mod_ref, _mod_row(i, TILE), 5)
    o_ref[...] = x_ref[...] + gt * _rms(yo, gpost_ref[0, 0])


def _merge(x, mod, y_ctx, y_lat, z, gate, v, pool, ccw, ccb, lng, lnb, wbc, pw, psc, wbp, sng, wbs, wout,
           norm_g, l):
    per_tile = TILE // HALO
    n_halo = T_ALL // HALO

    def cur(n):
        return pl.BlockSpec((TILE, n), lambda i: (i, 0))

    def trio(n):
        return [cur(n),
                pl.BlockSpec((HALO, n), lambda i: (jnp.maximum(i * per_tile - 1, 0), 0)),
                pl.BlockSpec((HALO, n), lambda i: (jnp.minimum((i + 1) * per_tile, n_halo - 1), 0))]

    return pl.pallas_call(
        _merge_kernel,
        grid=(N_TILES,),
        in_specs=[
            cur(D_MODEL),
            _layer_block(l, (8, N_MOD * D_MODEL)),
            pl.BlockSpec((TILE, SSD_D_INNER), lambda i: (jnp.minimum(i, CTX_TILES - 1), 0)),
            pl.BlockSpec((TILE, SSD_D_INNER), lambda i: (jnp.maximum(i - CTX_TILES, 0), 0)),
            cur(SSD_D_INNER), cur(3 * D_MODEL)]
        + trio(CONF_D) + trio(POOL_D) + [
            _layer_block(l, (CONF_KERNEL, CONF_D)), _layer_block(l, (1, CONF_D)),
            _layer_block(l, (1, CONF_D)), _layer_block(l, (1, CONF_D)),
            _layer_block(l, (CONF_D, D_MODEL)),
            _layer_block(l, (len(POOL_WINDOWS), 128, 128)), _layer_block(l, (1, POOL_D)),
            _layer_block(l, (POOL_D, D_MODEL)),
            _layer_block(l, (1, SSD_D_INNER)),
            _layer_block(l, (SSD_D_INNER, D_MODEL)),
            _layer_block(l, (D_MODEL, D_MODEL)),
            pl.BlockSpec((1, 1, 1, D_MODEL), lambda i: (l, 3, 0, 0), pipeline_mode=pl.Buffered(1)),
        ],
        out_specs=cur(D_MODEL),
        out_shape=jax.ShapeDtypeStruct((T_ALL, D_MODEL), F32),
        scratch_shapes=[pltpu.VMEM((TILE + 2 * HALO, CONF_D), F32),
                        pltpu.VMEM((7, CONF_SHIFT_ROWS, CONF_D), F32),
                        pltpu.VMEM((TILE + 2 * HALO, POOL_D), F32),
                        pltpu.VMEM((TILE, CONF_D), F32),
                        pltpu.VMEM((TILE, POOL_D), F32)],
        compiler_params=_cparams(("arbitrary",)),
        name="mixer_merge",
    )(x, mod, y_ctx, y_lat, z, gate, v, v, v, pool, pool, pool,
      ccw, ccb, lng, lnb, wbc, pw, psc, wbp, sng, wbs, wout, norm_g)


def _pos_embed_2d(n_tokens):
    rows = n_tokens // GRID_W
    r, col = jnp.meshgrid(jnp.arange(rows), jnp.arange(GRID_W), indexing='ij')
    r = r.reshape(-1).astype(F32)
    col = col.reshape(-1).astype(F32)
    q = D_MODEL // 4
    omega = 1.0 / (10000.0 ** (jnp.arange(q, dtype=F32) / q))
    ar = r[:, None] * omega
    ac = col[:, None] * omega
    return jnp.concatenate([jnp.sin(ar), jnp.cos(ar), jnp.sin(ac), jnp.cos(ac)], axis=-1)


def _pad_lanes(a, n):
    return jnp.pad(a, [(0, 0)] * (a.ndim - 1) + [(0, n - a.shape[-1])])


def kernel(x_prompt, x_sample, state_ssd, c, c_ctx, w_mod, b_mod, norm_g, w_ffn_in, w_ffn_out, w_in,
           ssd_conv_w, ssd_conv_b, ssd_a_log, ssd_dt_bias, ssd_d, ssd_norm_g, w_br_ssd, conf_conv_w,
           conf_conv_b, conf_ln_g, conf_ln_b, w_br_conf, pool_w, pool_scale, w_br_pool, w_out):
    xs = x_sample + _pos_embed_2d(DEC_SEQ).astype(x_sample.dtype)[None]
    x = (x_prompt.reshape(T_CTX, D_MODEL), xs.reshape(T_LAT, D_MODEL))

    cond_t = jnp.concatenate([c_ctx[None, :], c, jnp.zeros((8 - 1 - DEC_BATCH, D_MODEL), F32)], axis=0).T
    mod = _modulation(cond_t, w_mod, b_mod)

    w_in_b = w_in.astype(BF16)
    w_dt = lax.optimization_barrier(w_in[:, :, OFF_XBC:OFF_DT])
    wdt_b = jnp.concatenate([_pad_lanes(w_dt[:, :, :SSD_HEADS], DT_LANES),
                             _pad_lanes(w_dt[:, :, SSD_HEADS:], DT_LANES)], axis=-1).astype(BF16)
    wdt_t_b = jnp.swapaxes(w_dt, 1, 2).astype(BF16)
    dtb_row = _pad_lanes(ssd_dt_bias, DT_LANES).reshape(DEPTH, 1, 2 * DT_LANES)
    dtb_col = ssd_dt_bias.reshape(DEPTH, 2 * SSD_HEADS, 1)
    wbs_b = w_br_ssd.astype(BF16)
    wbc_b = w_br_conf.astype(BF16)
    wbp_b = w_br_pool.astype(BF16)
    wout_b = w_out.astype(BF16)
    pw_b = pool_w.astype(BF16)
    ng = norm_g.reshape(DEPTH, 6, 1, D_MODEL)
    alr = _pad_lanes(ssd_a_log, DT_LANES).reshape(DEPTH, 2, 1, DT_LANES)
    alc = ssd_a_log.reshape(DEPTH, 2, SSD_HEADS, 1)
    dskip = jnp.broadcast_to(jnp.repeat(ssd_d, SSD_HEAD_DIM, axis=1)[:, :, None],
                             (DEPTH, SSD_D_INNER, SSD_D_STATE))
    idx = jnp.arange(SSD_CHUNK)
    tri = jnp.stack([idx[:, None] >= idx[None, :], idx[:, None] <= idx[None, :]]).astype(BF16)
    state5 = state_ssd.reshape(DEC_BATCH, DEPTH, 2, SSD_D_INNER, SSD_D_STATE)
    ns_all = None

    def row1(a):
        return a.reshape(DEPTH, 1, a.shape[-1])

    for l in range(DEPTH):
        x = _ffn(x, mod, ng, w_ffn_in, w_ffn_out, l, 0)
        z, xbc, v, pool, gate, dt, dt_t = _proj(x, mod, ng, w_in_b, wdt_b, wdt_t_b, dtb_row, dtb_col, l)
        scan_args = (xbc, dt, dt_t, tri, alr, alc, dskip, ssd_conv_w, row1(ssd_conv_b))
        y_ctx, ns_all = _scan(*scan_args, None, ns_all, l, latent=False)
        y_lat = _scan(*scan_args, state5, None, l, latent=True)
        x = _merge(x, mod, y_ctx, y_lat, z, gate, v, pool, conf_conv_w, row1(conf_conv_b),
                   row1(conf_ln_g), row1(conf_ln_b), wbc_b, pw_b, row1(pool_scale), wbp_b,
                   row1(ssd_norm_g), wbs_b, wout_b, ng, l)
        x = _ffn(x, mod, ng, w_ffn_in, w_ffn_out, l, 1, split_out=(l == DEPTH - 1))

    y_prompt = x[0].reshape(BATCH, SEQ, D_MODEL)
    y_sample = x[1].reshape(DEC_BATCH, DEC_SEQ, D_MODEL)
    new_state = ns_all.reshape(BATCH, DEPTH, 2, SSD_HEADS, SSD_HEAD_DIM, SSD_D_STATE).astype(x_prompt.dtype)
    return (y_prompt, y_sample, new_state)
```

```python
import functools

import jax
import jax.numpy as jnp
from jax import lax
from jax.experimental import pallas as pl
from jax.experimental.pallas import tpu as pltpu

F32 = jnp.float32
BF16 = jnp.bfloat16

D_MODEL = 1024
BATCH = 32
SEQ = 256
DEPTH = 4
DEC_BATCH = 2
DEC_SEQ = 1024
GRID_W = 64
SSD_D_INNER = 1024
SSD_HEAD_DIM = 64
SSD_HEADS = 16
SSD_GROUPS = 4
SSD_D_STATE = 128
SSD_CONV = 5
SSD_CHUNK = 128
SSD_XBC = 2048
CONF_D = 512
CONF_KERNEL = 31
POOL_D = 512
POOL_WINDOWS = (2, 4, 8, 16)
D_FF = 2816
N_MOD = 9
FFN_RES = 0.5
EPS = 1e-6
OFF_XBC = 3072
OFF_DT = 3104
IN_COLS = 7712

T_CTX = BATCH * SEQ
T_LAT = DEC_BATCH * DEC_SEQ
T_ALL = T_CTX + T_LAT
TILE = 256
HALO = 16
N_TILES = T_ALL // TILE
CTX_TILES = T_CTX // TILE
LAT_TILES = DEC_SEQ // TILE
HEADS_PER_GROUP = SSD_HEADS // SSD_GROUPS
GROUP_ROWS = HEADS_PER_GROUP * SSD_HEAD_DIM
BC_COLS = 2 * SSD_GROUPS * SSD_D_STATE

FF_CHUNK = 256
N_FF_CHUNKS = D_FF // FF_CHUNK
TM_FFN = 512

N_WA = OFF_XBC
N_WB = IN_COLS - OFF_DT
PB_POOL = 2 * CONF_D
PB_GATE = PB_POOL + POOL_D
DT_LANES = 128

VMEM_LIMIT = 56 * 1024 * 1024


def _cparams(sem):
    return pltpu.CompilerParams(dimension_semantics=sem, vmem_limit_bytes=VMEM_LIMIT)


def _layer_block(l, shape):
    nd = len(shape)
    return pl.BlockSpec((1,) + tuple(shape), lambda *_: (l,) + (0,) * nd, pipeline_mode=pl.Buffered(1))


def _rms(x, g):
    ms = jnp.mean(x * x, axis=-1, keepdims=True)
    return x * lax.rsqrt(ms + EPS) * g


def _sigmoid(x):
    return 0.5 * jnp.tanh(0.5 * x) + 0.5


def _silu(x):
    u = 0.5 * x
    return u * jnp.tanh(u) + u


def _softplus(x):
    return jnp.maximum(x, 0.0) + jnp.log(1.0 + jnp.exp(-jnp.abs(x)))


def _mod_row(i, tm):
    ctx_tiles = T_CTX // tm
    per_seq = DEC_SEQ // tm
    return jnp.where(i < ctx_tiles, 0, 1 + jnp.maximum(i - ctx_tiles, 0) // per_seq)


def _mod_vec(mod_ref, row, k):
    return mod_ref[0, pl.ds(row, 1), pl.ds(k * D_MODEL, D_MODEL)]


def _mod_kernel(ct_ref, w_ref, b_ref, o_ref):
    ct = ct_ref[...]
    s = _silu(ct)
    w = w_ref[0]
    b = b_ref[0]
    o_ref[0] = jnp.zeros(o_ref.shape[1:], F32)
    for r in range(1 + DEC_BATCH):
        o_ref[0, r:r + 1, :] = jnp.sum(s[:, r:r + 1] * w, axis=0, keepdims=True) + b


def _modulation(cond_t, w_mod, b_mod):
    tn = 1024
    n_cols = N_MOD * D_MODEL
    return pl.pallas_call(
        _mod_kernel,
        grid=(DEPTH, n_cols // tn),
        in_specs=[
            pl.BlockSpec((D_MODEL, 8), lambda l, j: (0, 0)),
            pl.BlockSpec((1, D_MODEL, tn), lambda l, j: (l, 0, j)),
            pl.BlockSpec((1, 1, tn), lambda l, j: (l, 0, j)),
        ],
        out_specs=pl.BlockSpec((1, 8, tn), lambda l, j: (l, 0, j)),
        out_shape=jax.ShapeDtypeStruct((DEPTH, 8, n_cols), F32),
        compiler_params=_cparams(("arbitrary", "arbitrary")),
        name="modulation",
    )(cond_t, w_mod, b_mod.reshape(DEPTH, 1, n_cols))


def _ffn_kernel(*refs, tm, k0, l, f, split_in, split_out):
    refs = list(refs)
    x_refs = [refs.pop(0) for _ in range(2 if split_in else 1)]
    mod_ref, gpre_ref, gpost_ref, wi_hbm, wo_hbm = refs[:5]
    n_out = 2 if split_out else 1
    o_refs = refs[5:5 + n_out]
    acc_ref, wi_ref, wo_ref, sg_ref, su_ref, so_ref, sem = refs[5 + n_out:]
    i = pl.program_id(0)
    is_ctx = i < T_CTX // tm
    row = _mod_row(i, tm)
    x = jnp.where(is_ctx, x_refs[0][...], x_refs[1][...]) if split_in else x_refs[0][...]
    sh = _mod_vec(mod_ref, row, k0)
    sc = _mod_vec(mod_ref, row, k0 + 1)
    gt = _mod_vec(mod_ref, row, k0 + 2)
    h = (_rms(x, gpre_ref[0, 0]) * (1.0 + sc) + sh).astype(BF16)

    def chunk_copies(c):
        lo, slot = c * FF_CHUNK, c % 2
        return (pltpu.make_async_copy(wi_hbm.at[l, f, :, pl.ds(lo, FF_CHUNK)], sg_ref.at[slot], sem.at[0, slot]),
                pltpu.make_async_copy(wi_hbm.at[l, f, :, pl.ds(D_FF + lo, FF_CHUNK)], su_ref.at[slot],
                                      sem.at[1, slot]),
                pltpu.make_async_copy(wo_hbm.at[l, f, pl.ds(lo, FF_CHUNK), :], so_ref.at[slot], sem.at[2, slot]))

    def run(stream_weights):
        if stream_weights:
            for cp in chunk_copies(0):
                cp.start()
        for c in range(N_FF_CHUNKS):
            lo = c * FF_CHUNK
            if stream_weights:
                if c + 1 < N_FF_CHUNKS:
                    for cp in chunk_copies(c + 1):
                        cp.start()
                for cp in chunk_copies(c):
                    cp.wait()
                slot = c % 2
                wi_ref[:, lo:lo + FF_CHUNK] = sg_ref[slot].astype(BF16)
                wi_ref[:, D_FF + lo:D_FF + lo + FF_CHUNK] = su_ref[slot].astype(BF16)
                wo_ref[lo:lo + FF_CHUNK, :] = so_ref[slot].astype(BF16)
            g = jnp.dot(h, wi_ref[:, lo:lo + FF_CHUNK], preferred_element_type=F32)
            u = jnp.dot(h, wi_ref[:, D_FF + lo:D_FF + lo + FF_CHUNK], preferred_element_type=F32)
            a = (_silu(g) * u).astype(BF16)
            part = jnp.dot(a, wo_ref[lo:lo + FF_CHUNK, :], preferred_element_type=F32)
            if c == 0:
                acc_ref[...] = part
            else:
                acc_ref[...] += part
        out = x + (FFN_RES * gt) * _rms(acc_ref[...], gpost_ref[0, 0])
        if split_out:
            @pl.when(is_ctx)
            def _():
                o_refs[0][...] = out

            @pl.when(jnp.logical_not(is_ctx))
            def _():
                o_refs[1][...] = out
        else:
            o_refs[0][...] = out

    pl.when(i == 0)(functools.partial(run, True))
    pl.when(i != 0)(functools.partial(run, False))


def _ffn(xs, mod, norm_g, w_ffn_in, w_ffn_out, l, f, split_out=False):
    tm = TM_FFN
    ctx_tiles = T_CTX // tm
    split_in = isinstance(xs, tuple)

    def lf_block(shape, k):
        return pl.BlockSpec((1, 1) + shape, lambda i: (l, k, 0, 0), pipeline_mode=pl.Buffered(1))

    merged = pl.BlockSpec((tm, D_MODEL), lambda i: (i, 0))
    ctx_part = pl.BlockSpec((tm, D_MODEL), lambda i: (jnp.minimum(i, ctx_tiles - 1), 0))
    lat_part = pl.BlockSpec((tm, D_MODEL), lambda i: (jnp.maximum(i - ctx_tiles, 0), 0))
    parts_shape = [jax.ShapeDtypeStruct((T_CTX, D_MODEL), F32), jax.ShapeDtypeStruct((T_LAT, D_MODEL), F32)]
    return pl.pallas_call(
        functools.partial(_ffn_kernel, tm=tm, k0=6 * f, l=l, f=f, split_in=split_in, split_out=split_out),
        grid=(T_ALL // tm,),
        in_specs=([ctx_part, lat_part] if split_in else [merged]) + [
            _layer_block(l, (8, N_MOD * D_MODEL)),
            lf_block((1, D_MODEL), 4 * f),
            lf_block((1, D_MODEL), 4 * f + 1),
            pl.BlockSpec(memory_space=pl.ANY),
            pl.BlockSpec(memory_space=pl.ANY),
        ],
        out_specs=[ctx_part, lat_part] if split_out else merged,
        out_shape=parts_shape if split_out else jax.ShapeDtypeStruct((T_ALL, D_MODEL), F32),
        scratch_shapes=[pltpu.VMEM((tm, D_MODEL), F32),
                        pltpu.VMEM((D_MODEL, 2 * D_FF), BF16),
                        pltpu.VMEM((D_FF, D_MODEL), BF16),
                        pltpu.VMEM((2, D_MODEL, FF_CHUNK), F32),
                        pltpu.VMEM((2, D_MODEL, FF_CHUNK), F32),
                        pltpu.VMEM((2, FF_CHUNK, D_MODEL), F32),
                        pltpu.SemaphoreType.DMA((3, 2))],
        compiler_params=_cparams(("arbitrary",)),
        name="ffn",
    )(*(xs if split_in else (xs,)), mod, norm_g, norm_g, w_ffn_in, w_ffn_out)


def _proj_kernel(x_ref, xp_ref, xn_ref, mod_ref, g_ref, w_ref, wdt_ref, wdt_t_ref, dtb_row_ref, dtb_col_ref,
                 scw_ref, scb_ref,
                 z_ref, xs_ref, bc_ref, v_ref, pool_ref, gate_ref, dt_ref, dt_t_ref, wb_ref, xpad_ref):
    i = pl.program_id(0)
    lat = i >= CTX_TILES
    k = jnp.maximum(i - CTX_TILES, 0) % LAT_TILES
    has_prev = jnp.logical_and(lat, k != 0)
    has_next = jnp.logical_and(lat, k != LAT_TILES - 1)
    row = _mod_row(i, TILE)
    sh = _mod_vec(mod_ref, row, 3)
    sc = _mod_vec(mod_ref, row, 4)
    x_all = jnp.concatenate([x_ref[...], xp_ref[...], xn_ref[...]], axis=0)
    h_all = (_rms(x_all, g_ref[0, 0]) * (1.0 + sc) + sh).astype(BF16)
    h = h_all[:TILE]

    @pl.when(i == 0)
    def _():
        wb_ref[...] = w_ref[0, :, OFF_DT:IN_COLS]

    def mm(w, lo, hi):
        return jnp.dot(h, w[:, lo:hi], preferred_element_type=F32)

    wa = w_ref.at[0]
    xbc = jnp.dot(h_all, wa[:, SSD_D_INNER:N_WA], preferred_element_type=F32)
    xpad_ref[HALO:HALO + TILE, :] = xbc[:TILE]
    xpad_ref[0:HALO, :] = jnp.where(has_prev, xbc[TILE:TILE + HALO], 0.0)
    xpad_ref[HALO + TILE:, :] = jnp.where(has_next, xbc[TILE + HALO:], 0.0)
    for cb in range(SSD_XBC // 128):
        col = slice(cb * 128, (cb + 1) * 128)
        first = HALO - SSD_CONV // 2
        acc = scb_ref[0, :, col]
        for t in range(SSD_CONV):
            acc = acc + scw_ref[0, t:t + 1, col] * xpad_ref[first + t:first + t + TILE, col]
        if cb < SSD_D_INNER // 128:
            xs_ref[:, col] = _silu(acc)
        else:
            bc_ref[:, cb * 128 - SSD_D_INNER:(cb + 1) * 128 - SSD_D_INNER] = _silu(acc).astype(BF16)

    z_ref[...] = _silu(mm(wa, 0, SSD_D_INNER))
    ag = mm(wb_ref, 0, PB_POOL)
    v_ref[...] = ag[:, :CONF_D] * _sigmoid(ag[:, CONF_D:])
    pool_ref[...] = mm(wb_ref, PB_POOL, PB_GATE)
    gate_ref[...] = _sigmoid(mm(wb_ref, PB_GATE, N_WB))
    dt_ref[...] = _softplus(jnp.dot(h, wdt_ref[0], preferred_element_type=F32) + dtb_row_ref[0])
    dt_t = lax.dot_general(wdt_t_ref[0], h, (((1,), (1,)), ((), ())), preferred_element_type=F32)
    dt_t_ref[...] = _softplus(dt_t + dtb_col_ref[0])


def _proj(x, mod, norm_g, w_b, wdt, wdt_t, dtb_row, dtb_col, scw, scb, l):
    per_tile = TILE // HALO
    n_halo = T_ALL // HALO

    def tile(n):
        return pl.BlockSpec((TILE, n), lambda i: (i, 0))

    outs = ((SSD_D_INNER, F32), (SSD_D_INNER, F32), (BC_COLS, BF16), (CONF_D, F32), (POOL_D, F32),
            (3 * D_MODEL, F32), (2 * DT_LANES, F32))
    return pl.pallas_call(
        _proj_kernel,
        grid=(N_TILES,),
        in_specs=[
            tile(D_MODEL),
            pl.BlockSpec((HALO, D_MODEL), lambda i: (jnp.maximum(i * per_tile - 1, 0), 0)),
            pl.BlockSpec((HALO, D_MODEL), lambda i: (jnp.minimum((i + 1) * per_tile, n_halo - 1), 0)),
            _layer_block(l, (8, N_MOD * D_MODEL)),
            pl.BlockSpec((1, 1, 1, D_MODEL), lambda i: (l, 2, 0, 0), pipeline_mode=pl.Buffered(1)),
            _layer_block(l, (D_MODEL, IN_COLS)),
            _layer_block(l, (D_MODEL, 2 * DT_LANES)),
            _layer_block(l, (2 * SSD_HEADS, D_MODEL)),
            _layer_block(l, (1, 2 * DT_LANES)),
            _layer_block(l, (2 * SSD_HEADS, 1)),
            _layer_block(l, (SSD_CONV, SSD_XBC)),
            _layer_block(l, (1, SSD_XBC)),
        ],
        out_specs=[tile(n) for n, _ in outs] + [pl.BlockSpec((2 * SSD_HEADS, TILE), lambda i: (0, i))],
        out_shape=[jax.ShapeDtypeStruct((T_ALL, n), dt) for n, dt in outs]
        + [jax.ShapeDtypeStruct((2 * SSD_HEADS, T_ALL), F32)],
        scratch_shapes=[pltpu.VMEM((D_MODEL, N_WB), BF16),
                        pltpu.VMEM((TILE + 2 * HALO, SSD_XBC), F32)],
        compiler_params=_cparams(("arbitrary",)),
        name="in_proj",
    )(x, x, x, mod, norm_g, w_b, wdt, wdt_t, dtb_row, dtb_col, scw, scb)


_NT = (((1,), (1,)), ((), ()))


def _split3(v):
    p0 = v.astype(BF16)
    r = v - p0.astype(F32)
    p1 = r.astype(BF16)
    p2 = (r - p1.astype(F32)).astype(BF16)
    return p0, p1, p2


def _scan_kernel(*refs, nc, latent, first_layer):
    (xs_ref, bcs_ref, dt_ref, dtt_ref, tri_ref, alr_ref, alc_ref, dsk_ref) = refs[:8]
    refs = refs[8:]
    if latent:
        h0_ref, y_ref = refs[:2]
        refs = refs[2:]
    elif first_layer:
        y_ref, ns_ref = refs[:2]
        refs = refs[2:]
    else:
        y_ref, ns_ref = refs[1:3]
        refs = refs[3:]
    xt_ref, acs_ref, acst_ref, st_ref, s_ref, yt_ref = refs

    def rows_of(c):
        return pl.ds(pl.multiple_of(c * SSD_CHUNK, SSD_CHUNK), SSD_CHUNK)

    def dir_rows(d):
        return slice(d * SSD_HEADS, (d + 1) * SSD_HEADS)

    def b_cols(g):
        return slice(g * SSD_D_STATE, (g + 1) * SSD_D_STATE)

    def c_cols(g):
        lo = SSD_GROUPS * SSD_D_STATE + g * SSD_D_STATE
        return slice(lo, lo + SSD_D_STATE)

    def head_rows(h):
        return slice(h * SSD_HEAD_DIM, (h + 1) * SSD_HEAD_DIM)

    def total_col(acs_t, d):
        return acs_t[:, SSD_CHUNK - 1:] if d == 0 else acs_t[:, :1]

    def cumsums(c, carry):
        rows = rows_of(c)
        for d in range(2):
            tri = tri_ref[d]
            a_row = -jnp.exp(alr_ref[0, d])
            a_col = -jnp.exp(alc_ref[0, d])
            p = jnp.concatenate(_split3(dt_ref[rows, d * DT_LANES:(d + 1) * DT_LANES]), axis=1)
            r = jnp.dot(tri, p, preferred_element_type=F32)
            acs_ref[d, rows, :] = (r[:, :128] + r[:, 128:256] + r[:, 256:]) * a_row
            q = jnp.concatenate(_split3(dtt_ref[dir_rows(d), rows]), axis=0)
            rt = lax.dot_general(q, tri, _NT, preferred_element_type=F32)
            acst_ref[d, :, rows] = (rt[:SSD_HEADS] + rt[SSD_HEADS:2 * SSD_HEADS] + rt[2 * SSD_HEADS:]) * a_col
        return carry

    lax.fori_loop(0, nc, cumsums, 0)

    def local_states(c, carry):
        rows = rows_of(c)
        xt = xs_ref[rows, :].T
        xt_ref[:, rows] = xt
        for d in range(2):
            acs_t = acst_ref[d, :, rows]
            w = dtt_ref[dir_rows(d), rows] * jnp.exp(total_col(acs_t, d) - acs_t)
            for g in range(SSD_GROUPS):
                bg = bcs_ref[rows, b_cols(g)]
                parts = [(xt[head_rows(h), :] * w[h:h + 1, :]).astype(BF16)
                         for h in range(g * HEADS_PER_GROUP, (g + 1) * HEADS_PER_GROUP)]
                st_ref[d, c, g * GROUP_ROWS:(g + 1) * GROUP_ROWS, :] = jnp.dot(
                    jnp.concatenate(parts, axis=0), bg, preferred_element_type=F32)
        return carry

    lax.fori_loop(0, nc, local_states, 0)

    for d in range(2):
        if latent:
            s_ref[...] = h0_ref[0, 0, d]
        else:
            s_ref[...] = jnp.zeros(s_ref.shape, F32)

        def recur(j, carry, d=d):
            c = j if d == 0 else nc - 1 - j
            acs_t = acst_ref[d, :, rows_of(c)]
            e_tot = jnp.broadcast_to(jnp.exp(total_col(acs_t, d)), (SSD_HEADS, SSD_D_STATE))
            for h in range(SSD_HEADS):
                s_old = s_ref[head_rows(h), :]
                cs = st_ref[d, c, head_rows(h), :]
                st_ref[d, c, head_rows(h), :] = s_old
                s_ref[head_rows(h), :] = s_old * e_tot[h:h + 1, :] + cs
            return carry

        lax.fori_loop(0, nc, recur, 0)
        if not latent:
            ns_ref[0, 0, d] = s_ref[...]
    if first_layer and not latent:
        ns_ref[0, 1:] = jnp.zeros((DEPTH - 1, 2, SSD_D_INNER, SSD_D_STATE), F32)

    s_idx = lax.broadcasted_iota(jnp.int32, (SSD_CHUNK, SSD_CHUNK), 0)
    l_idx = lax.broadcasted_iota(jnp.int32, (SSD_CHUNK, SSD_CHUNK), 1)
    visible = (s_idx <= l_idx, s_idx >= l_idx)

    def outputs(c, carry):
        rows = rows_of(c)
        acs_t = [acst_ref[d, :, rows] for d in range(2)]
        e_acs_t = [jnp.exp(a) for a in acs_t]
        dt_t = [dtt_ref[dir_rows(d), rows] for d in range(2)]
        for g in range(SSD_GROUPS):
            bg = bcs_ref[rows, b_cols(g)]
            cg = bcs_ref[rows, c_cols(g)]
            g_t = lax.dot_general(bg, cg, _NT, preferred_element_type=F32)
            y_in = [lax.dot_general(st_ref[d, c, g * GROUP_ROWS:(g + 1) * GROUP_ROWS, :].astype(BF16), cg, _NT,
                                    preferred_element_type=F32) for d in range(2)]
            for hg in range(HEADS_PER_GROUP):
                h = g * HEADS_PER_GROUP + hg
                x_h = xt_ref[head_rows(h), rows]
                lhs, rhs = [], []
                y_h = dsk_ref[0, head_rows(h), :] * x_h
                for d in range(2):
                    seg = jnp.where(visible[d], acs_t[d][h:h + 1, :] - acs_ref[d, rows, h:h + 1], -jnp.inf)
                    rhs.append((g_t * jnp.exp(seg)).astype(BF16))
                    lhs.append((x_h * dt_t[d][h:h + 1, :]).astype(BF16))
                    y_h = y_h + y_in[d][hg * SSD_HEAD_DIM:(hg + 1) * SSD_HEAD_DIM, :] * e_acs_t[d][h:h + 1, :]
                y_h = y_h + jnp.dot(jnp.concatenate(lhs, axis=1), jnp.concatenate(rhs, axis=0),
                                    preferred_element_type=F32)
                yt_ref[head_rows(h), :] = y_h
        y_ref[rows, :] = yt_ref[...].T
        return carry

    lax.fori_loop(0, nc, outputs, 0)


def _scan(xs, bc, dt, dt_t, tri, alr, alc, dskip, state_ssd5, ns_all, l, latent):
    seq = DEC_SEQ if latent else SEQ
    n_seq = DEC_BATCH if latent else BATCH
    first = T_CTX // seq if latent else 0
    nc = seq // SSD_CHUNK
    first_layer = ns_all is None
    in_specs = [
        pl.BlockSpec((seq, SSD_D_INNER), lambda i: (first + i, 0)),
        pl.BlockSpec((seq, BC_COLS), lambda i: (first + i, 0)),
        pl.BlockSpec((seq, 2 * DT_LANES), lambda i: (first + i, 0)),
        pl.BlockSpec((2 * SSD_HEADS, seq), lambda i: (0, first + i)),
        pl.BlockSpec((2, SSD_CHUNK, SSD_CHUNK), lambda i: (0, 0, 0), pipeline_mode=pl.Buffered(1)),
        _layer_block(l, (2, 1, DT_LANES)),
        _layer_block(l, (2, SSD_HEADS, 1)),
        _layer_block(l, (SSD_D_INNER, SSD_D_STATE)),
    ]
    args = (xs, bc, dt, dt_t, tri, alr, alc, dskip)
    scratch = [pltpu.VMEM((SSD_D_INNER, seq), F32),
               pltpu.VMEM((2, seq, DT_LANES), F32),
               pltpu.VMEM((2, SSD_HEADS, seq), F32),
               pltpu.VMEM((2, nc, SSD_D_INNER, SSD_D_STATE), F32),
               pltpu.VMEM((SSD_D_INNER, SSD_D_STATE), F32),
               pltpu.VMEM((SSD_D_INNER, SSD_CHUNK), F32)]
    kern = functools.partial(_scan_kernel, nc=nc, latent=latent, first_layer=first_layer)
    y_spec = pl.BlockSpec((seq, SSD_D_INNER), lambda i: (i, 0))
    if latent:
        state_block = (1, 1, 2, SSD_D_INNER, SSD_D_STATE)
        return pl.pallas_call(
            kern, grid=(n_seq,),
            in_specs=in_specs + [pl.BlockSpec(state_block, lambda i: (i, l, 0, 0, 0))],
            out_specs=y_spec,
            out_shape=jax.ShapeDtypeStruct((T_LAT, SSD_D_INNER), F32),
            scratch_shapes=scratch,
            compiler_params=_cparams(("arbitrary",)),
            name="ssd_scan_latent",
        )(*args, state_ssd5)
    ns_shape = jax.ShapeDtypeStruct((BATCH, DEPTH, 2, SSD_D_INNER, SSD_D_STATE), F32)
    y_shape = jax.ShapeDtypeStruct((T_CTX, SSD_D_INNER), F32)
    if first_layer:
        return pl.pallas_call(
            kern, grid=(n_seq,),
            in_specs=in_specs,
            out_specs=[y_spec, pl.BlockSpec((1, DEPTH, 2, SSD_D_INNER, SSD_D_STATE), lambda i: (i, 0, 0, 0, 0))],
            out_shape=[y_shape, ns_shape],
            scratch_shapes=scratch,
            compiler_params=_cparams(("arbitrary",)),
            name="ssd_scan_ctx",
        )(*args)
    return pl.pallas_call(
        kern, grid=(n_seq,),
        in_specs=in_specs + [pl.BlockSpec(memory_space=pl.ANY)],
        out_specs=[y_spec, pl.BlockSpec((1, 1, 2, SSD_D_INNER, SSD_D_STATE), lambda i: (i, l, 0, 0, 0))],
        out_shape=[y_shape, ns_shape],
        scratch_shapes=scratch,
        input_output_aliases={len(args): 1},
        compiler_params=_cparams(("arbitrary",)),
        name="ssd_scan_ctx",
    )(*args, ns_all)


CONF_SHIFT_ROWS = TILE + 3 * 8


def _fill_padded(pad_ref, cur_ref, prev_ref, next_ref, has_prev, has_next):
    pad_ref[0:HALO, :] = jnp.where(has_prev, prev_ref[...], 0.0)
    pad_ref[HALO:HALO + TILE, :] = cur_ref[...]
    pad_ref[HALO + TILE:, :] = jnp.where(has_next, next_ref[...], 0.0)


def _merge_kernel(x_ref, mod_ref, yc_ref, yl_ref, z_ref, gate_ref,
                  vc_ref, vp_ref, vn_ref, pc_ref, pp_ref, pn_ref,
                  ccw_ref, ccb_ref, lng_ref, lnb_ref, wbc_ref, pw_ref, psc_ref, wbp_ref,
                  sng_ref, wbs_ref, wout_ref, gpost_ref,
                  o_ref,
                  vpad_ref, vsh_ref, ppad_ref, cv_ref, mix_ref):
    i = pl.program_id(0)
    lat = i >= CTX_TILES
    k = jnp.maximum(i - CTX_TILES, 0) % LAT_TILES
    has_prev = jnp.logical_and(lat, k != 0)
    has_next = jnp.logical_and(lat, k != LAT_TILES - 1)
    _fill_padded(vpad_ref, vc_ref, vp_ref, vn_ref, has_prev, has_next)
    _fill_padded(ppad_ref, pc_ref, pp_ref, pn_ref, has_prev, has_next)

    y = jnp.where(lat, yl_ref[...], yc_ref[...])
    br_ssd = jnp.dot(_rms(y * z_ref[...], sng_ref[0]).astype(BF16), wbs_ref[0], preferred_element_type=F32)

    for s in range(1, 8):
        vsh_ref[s - 1] = vpad_ref[s:s + CONF_SHIFT_ROWS, :]

    for cb in range(CONF_D // 128):
        col = slice(cb * 128, (cb + 1) * 128)
        first = HALO - CONF_KERNEL // 2
        acc = ccb_ref[0, :, col]
        for t in range(CONF_KERNEL):
            off = first + t
            a, s = off // 8, off % 8
            src = vpad_ref if s == 0 else vsh_ref.at[s - 1]
            acc = acc + ccw_ref[0, t:t + 1, col] * src[8 * a:8 * a + TILE, col]
        cv_ref[:, col] = acc
    cv = cv_ref[...]
    mu = jnp.mean(cv, axis=-1, keepdims=True)
    cen = cv - mu
    var = jnp.mean(cen * cen, axis=-1, keepdims=True)
    ln = cen * lax.rsqrt(var + EPS) * lng_ref[0] + lnb_ref[0]
    br_conf = jnp.dot(_silu(ln).astype(BF16), wbc_ref[0], preferred_element_type=F32)

    seq_len = jnp.where(lat, DEC_SEQ, SEQ)
    pos = k * TILE + lax.broadcasted_iota(jnp.int32, (TILE, 1), 0)
    for gi, w in enumerate(POOL_WINDOWS):
        cols = slice(gi * 128, (gi + 1) * 128)
        first = HALO - w // 2
        s = ppad_ref[first:first + TILE, cols]
        for j in range(1, w):
            s = s + ppad_ref[first + j:first + j + TILE, cols]
        lo = jnp.maximum(pos - w // 2, 0)
        hi = jnp.minimum(pos - w // 2 + w, seq_len)
        pooled = s / (hi - lo).astype(F32) - pc_ref[:, cols]
        mixed = jnp.dot(pooled.astype(BF16), pw_ref[0, gi], preferred_element_type=F32)
        mix_ref[:, cols] = mixed * psc_ref[0, :, cols]
    br_pool = jnp.dot(mix_ref[...].astype(BF16), wbp_ref[0], preferred_element_type=F32)

    merged =(gate_ref[:, 0:D_MODEL] * br_ssd
              + gate_ref[:, D_MODEL:2 * D_MODEL] * br_conf
              + gate_ref[:, 2 * D_MODEL:] * br_pool)
    yo = jnp.dot(merged.astype(BF16), wout_ref[0], preferred_element_type=F32)
    gt = _mod_vec(mod_ref, _mod_row(i, TILE), 5)
    o_ref[...] = x_ref[...] + gt * _rms(yo, gpost_ref[0, 0])


def _merge(x, mod, y_ctx, y_lat, z, gate, v, pool, ccw, ccb, lng, lnb, wbc, pw, psc, wbp, sng, wbs, wout,
           norm_g, l):
    per_tile = TILE // HALO
    n_halo = T_ALL // HALO

    def cur(n):
        return pl.BlockSpec((TILE, n), lambda i: (i, 0))

    def trio(n):
        return [cur(n),
                pl.BlockSpec((HALO, n), lambda i: (jnp.maximum(i * per_tile - 1, 0), 0)),
                pl.BlockSpec((HALO, n), lambda i: (jnp.minimum((i + 1) * per_tile, n_halo - 1), 0))]

    return pl.pallas_call(
        _merge_kernel,
        grid=(N_TILES,),
        in_specs=[
            cur(D_MODEL),
            _layer_block(l, (8, N_MOD * D_MODEL)),
            pl.BlockSpec((TILE, SSD_D_INNER), lambda i: (jnp.minimum(i, CTX_TILES - 1), 0)),
            pl.BlockSpec((TILE, SSD_D_INNER), lambda i: (jnp.maximum(i - CTX_TILES, 0), 0)),
            cur(SSD_D_INNER), cur(3 * D_MODEL)]
        + trio(CONF_D) + trio(POOL_D) + [
            _layer_block(l, (CONF_KERNEL, CONF_D)), _layer_block(l, (1, CONF_D)),
            _layer_block(l, (1, CONF_D)), _layer_block(l, (1, CONF_D)),
            _layer_block(l, (CONF_D, D_MODEL)),
            _layer_block(l, (len(POOL_WINDOWS), 128, 128)), _layer_block(l, (1, POOL_D)),
            _layer_block(l, (POOL_D, D_MODEL)),
            _layer_block(l, (1, SSD_D_INNER)),
            _layer_block(l, (SSD_D_INNER, D_MODEL)),
            _layer_block(l, (D_MODEL, D_MODEL)),
            pl.BlockSpec((1, 1, 1, D_MODEL), lambda i: (l, 3, 0, 0), pipeline_mode=pl.Buffered(1)),
        ],
        out_specs=cur(D_MODEL),
        out_shape=jax.ShapeDtypeStruct((T_ALL, D_MODEL), F32),
        scratch_shapes=[pltpu.VMEM((TILE + 2 * HALO, CONF_D), F32),
                        pltpu.VMEM((7, CONF_SHIFT_ROWS, CONF_D), F32),
                        pltpu.VMEM((TILE + 2 * HALO, POOL_D), F32),
                        pltpu.VMEM((TILE, CONF_D), F32),
                        pltpu.VMEM((TILE, POOL_D), F32)],
        compiler_params=_cparams(("arbitrary",)),
        name="mixer_merge",
    )(x, mod, y_ctx, y_lat, z, gate, v, v, v, pool, pool, pool,
      ccw, ccb, lng, lnb, wbc, pw, psc, wbp, sng, wbs, wout, norm_g)


def _pos_embed_2d(n_tokens):
    rows = n_tokens // GRID_W
    r, col = jnp.meshgrid(jnp.arange(rows), jnp.arange(GRID_W), indexing='ij')
    r = r.reshape(-1).astype(F32)
    col = col.reshape(-1).astype(F32)
    q = D_MODEL // 4
    omega = 1.0 / (10000.0 ** (jnp.arange(q, dtype=F32) / q))
    ar = r[:, None] * omega
    ac = col[:, None] * omega
    return jnp.concatenate([jnp.sin(ar), jnp.cos(ar), jnp.sin(ac), jnp.cos(ac)], axis=-1)


def _pad_lanes(a, n):
    return jnp.pad(a, [(0, 0)] * (a.ndim - 1) + [(0, n - a.shape[-1])])


def kernel(x_prompt, x_sample, state_ssd, c, c_ctx, w_mod, b_mod, norm_g, w_ffn_in, w_ffn_out, w_in,
           ssd_conv_w, ssd_conv_b, ssd_a_log, ssd_dt_bias, ssd_d, ssd_norm_g, w_br_ssd, conf_conv_w,
           conf_conv_b, conf_ln_g, conf_ln_b, w_br_conf, pool_w, pool_scale, w_br_pool, w_out):
    xs = x_sample + _pos_embed_2d(DEC_SEQ).astype(x_sample.dtype)[None]
    x = (x_prompt.reshape(T_CTX, D_MODEL), xs.reshape(T_LAT, D_MODEL))

    cond_t = jnp.concatenate([c_ctx[None, :], c, jnp.zeros((8 - 1 - DEC_BATCH, D_MODEL), F32)], axis=0).T
    mod = _modulation(cond_t, w_mod, b_mod)

    w_in_b = w_in.astype(BF16)
    w_dt = lax.optimization_barrier(w_in[:, :, OFF_XBC:OFF_DT])
    wdt_b = jnp.concatenate([_pad_lanes(w_dt[:, :, :SSD_HEADS], DT_LANES),
                             _pad_lanes(w_dt[:, :, SSD_HEADS:], DT_LANES)], axis=-1).astype(BF16)
    wdt_t_b = jnp.swapaxes(w_dt, 1, 2).astype(BF16)
    dtb_row = _pad_lanes(ssd_dt_bias, DT_LANES).reshape(DEPTH, 1, 2 * DT_LANES)
    dtb_col = ssd_dt_bias.reshape(DEPTH, 2 * SSD_HEADS, 1)
    wbs_b = w_br_ssd.astype(BF16)
    wbc_b = w_br_conf.astype(BF16)
    wbp_b = w_br_pool.astype(BF16)
    wout_b = w_out.astype(BF16)
    pw_b = pool_w.astype(BF16)
    ng = norm_g.reshape(DEPTH, 6, 1, D_MODEL)
    alr = _pad_lanes(ssd_a_log, DT_LANES).reshape(DEPTH, 2, 1, DT_LANES)
    alc = ssd_a_log.reshape(DEPTH, 2, SSD_HEADS, 1)
    dskip = jnp.broadcast_to(jnp.repeat(ssd_d, SSD_HEAD_DIM, axis=1)[:, :, None],
                             (DEPTH, SSD_D_INNER, SSD_D_STATE))
    idx = jnp.arange(SSD_CHUNK)
    tri = jnp.stack([idx[:, None] >= idx[None, :], idx[:, None] <= idx[None, :]]).astype(BF16)
    state5 = state_ssd.reshape(DEC_BATCH, DEPTH, 2, SSD_D_INNER, SSD_D_STATE)
    ns_all = None

    def row1(a):
        return a.reshape(DEPTH, 1, a.shape[-1])

    for l in range(DEPTH):
        x = _ffn(x, mod, ng, w_ffn_in, w_ffn_out, l, 0)
        z, xs_c, bc_c, v, pool, gate, dt, dt_t = _proj(x, mod, ng, w_in_b, wdt_b, wdt_t_b, dtb_row, dtb_col,
                                                       ssd_conv_w, row1(ssd_conv_b), l)
        scan_args = (xs_c, bc_c, dt, dt_t, tri, alr, alc, dskip)
        y_ctx, ns_all = _scan(*scan_args, None, ns_all, l, latent=False)
        y_lat = _scan(*scan_args, state5, None, l, latent=True)
        x = _merge(x, mod, y_ctx, y_lat, z, gate, v, pool, conf_conv_w, row1(conf_conv_b),
                   row1(conf_ln_g), row1(conf_ln_b), wbc_b, pw_b, row1(pool_scale), wbp_b,
                   row1(ssd_norm_g), wbs_b, wout_b, ng, l)
        x = _ffn(x, mod, ng, w_ffn_in, w_ffn_out, l, 1, split_out=(l == DEPTH - 1))

    y_prompt = x[0].reshape(BATCH, SEQ, D_MODEL)
    y_sample = x[1].reshape(DEC_BATCH, DEC_SEQ, D_MODEL)
    new_state = ns_all.reshape(BATCH, DEPTH, 2, SSD_HEADS, SSD_HEAD_DIM, SSD_D_STATE).astype(x_prompt.dtype)
    return (y_prompt, y_sample, new_state)
```

```python
import functools

import jax
import jax.numpy as jnp
from jax import lax
from jax.experimental import pallas as pl
from jax.experimental.pallas import tpu as pltpu

F32 = jnp.float32
BF16 = jnp.bfloat16

D_MODEL = 1024
BATCH = 32
SEQ = 256
DEPTH = 4
DEC_BATCH = 2
DEC_SEQ = 1024
GRID_W = 64
SSD_D_INNER = 1024
SSD_HEAD_DIM = 64
SSD_HEADS = 16
SSD_GROUPS = 4
SSD_D_STATE = 128
SSD_CONV = 5
SSD_CHUNK = 128
SSD_XBC = 2048
CONF_D = 512
CONF_KERNEL = 31
POOL_D = 512
POOL_WINDOWS = (2, 4, 8, 16)
D_FF = 2816
N_MOD = 9
FFN_RES = 0.5
EPS = 1e-6
OFF_XBC = 3072
OFF_DT = 3104
IN_COLS = 7712

T_CTX = BATCH * SEQ
T_LAT = DEC_BATCH * DEC_SEQ
T_ALL = T_CTX + T_LAT
TILE = 256
HALO = 16
N_TILES = T_ALL // TILE
CTX_TILES = T_CTX // TILE
LAT_TILES = DEC_SEQ // TILE
HEADS_PER_GROUP = SSD_HEADS // SSD_GROUPS
GROUP_ROWS = HEADS_PER_GROUP * SSD_HEAD_DIM
BC_COLS = 2 * SSD_GROUPS * SSD_D_STATE

FF_CHUNK = 256
N_FF_CHUNKS = D_FF // FF_CHUNK
TM_FFN = 512
FFN_PARTS = 1

N_WA = OFF_XBC
N_WB = IN_COLS - OFF_DT
PB_POOL = 2 * CONF_D
PB_GATE = PB_POOL + POOL_D
DT_LANES = 128
CONV_ROWS = TILE
W_SLAB = 128

VMEM_LIMIT = 56 * 1024 * 1024


def _cparams(sem):
    return pltpu.CompilerParams(dimension_semantics=sem, vmem_limit_bytes=VMEM_LIMIT)


def _layer_block(l, shape):
    nd = len(shape)
    return pl.BlockSpec((1,) + tuple(shape), lambda *_: (l,) + (0,) * nd, pipeline_mode=pl.Buffered(1))


def _rms(x, g):
    ms = jnp.mean(x * x, axis=-1, keepdims=True)
    return x * lax.rsqrt(ms + EPS) * g


def _sigmoid(x):
    return 0.5 * jnp.tanh(0.5 * x) + 0.5


def _silu(x):
    u = 0.5 * x
    return u * jnp.tanh(u) + u


def _softplus(x):
    return jnp.maximum(x, 0.0) + jnp.log(1.0 + jnp.exp(-jnp.abs(x)))


def _mod_row(i, tm):
    ctx_tiles = T_CTX // tm
    per_seq = DEC_SEQ // tm
    return jnp.where(i < ctx_tiles, 0, 1 + jnp.maximum(i - ctx_tiles, 0) // per_seq)


def _mod_vec(mod_ref, row, k):
    return mod_ref[0, pl.ds(row, 1), pl.ds(k * D_MODEL, D_MODEL)]


def _mod_kernel(ct_ref, w_ref, b_ref, o_ref):
    ct = ct_ref[...]
    s = _silu(ct)
    w = w_ref[0]
    b = b_ref[0]
    o_ref[0] = jnp.zeros(o_ref.shape[1:], F32)
    for r in range(1 + DEC_BATCH):
        o_ref[0, r:r + 1, :] = jnp.sum(s[:, r:r + 1] * w, axis=0, keepdims=True) + b


def _modulation(cond_t, w_mod, b_mod):
    tn = 3072
    n_cols = N_MOD * D_MODEL
    return pl.pallas_call(
        _mod_kernel,
        grid=(DEPTH, n_cols // tn),
        in_specs=[
            pl.BlockSpec((D_MODEL, 8), lambda l, j: (0, 0)),
            pl.BlockSpec((1, D_MODEL, tn), lambda l, j: (l, 0, j)),
            pl.BlockSpec((1, 1, tn), lambda l, j: (l, 0, j)),
        ],
        out_specs=pl.BlockSpec((1, 8, tn), lambda l, j: (l, 0, j)),
        out_shape=jax.ShapeDtypeStruct((DEPTH, 8, n_cols), F32),
        compiler_params=_cparams(("arbitrary", "arbitrary")),
        name="modulation",
    )(cond_t, w_mod, b_mod.reshape(DEPTH, 1, n_cols))


def _ffn_kernel(*refs, tm, k0, l, f, split_in, split_out):
    refs = list(refs)
    x_refs = [refs.pop(0) for _ in range(2 if split_in else 1)]
    mod_ref, gpre_ref, gpost_ref, wi_hbm, wo_hbm = refs[:5]
    n_out = 2 if split_out else 1
    o_refs = refs[5:5 + n_out]
    acc_ref, wi_ref, wo_ref, sg_ref, su_ref, so_ref, sem = refs[5 + n_out:]
    i = pl.program_id(0)
    is_ctx = i < T_CTX // tm
    row = _mod_row(i, tm)
    sh = _mod_vec(mod_ref, row, k0)
    sc = _mod_vec(mod_ref, row, k0 + 1)
    gt = _mod_vec(mod_ref, row, k0 + 2)

    def load_x(rows):
        if split_in:
            return jnp.where(is_ctx, x_refs[0][rows, :], x_refs[1][rows, :])
        return x_refs[0][rows, :]

    def chunk_copies(c):
        lo, slot = c * FF_CHUNK, c % 2
        return (pltpu.make_async_copy(wi_hbm.at[l, f, :, pl.ds(lo, FF_CHUNK)], sg_ref.at[slot], sem.at[0, slot]),
                pltpu.make_async_copy(wi_hbm.at[l, f, :, pl.ds(D_FF + lo, FF_CHUNK)], su_ref.at[slot],
                                      sem.at[1, slot]),
                pltpu.make_async_copy(wo_hbm.at[l, f, pl.ds(lo, FF_CHUNK), :], so_ref.at[slot], sem.at[2, slot]))

    def run(stream_weights):
        if stream_weights:
            for cp in chunk_copies(0):
                cp.start()
        n_parts = 1 if stream_weights else FFN_PARTS
        pm = tm // n_parts
        for p in range(n_parts):
            rows = slice(p * pm, (p + 1) * pm)
            x = load_x(rows)
            h = (_rms(x, gpre_ref[0, 0]) * (1.0 + sc) + sh).astype(BF16)
            for c in range(N_FF_CHUNKS):
                lo = c * FF_CHUNK
                if stream_weights:
                    if c + 1 < N_FF_CHUNKS:
                        for cp in chunk_copies(c + 1):
                            cp.start()
                    for cp in chunk_copies(c):
                        cp.wait()
                    slot = c % 2
                    wi_ref[:, lo:lo + FF_CHUNK] = sg_ref[slot].astype(BF16)
                    wi_ref[:, D_FF + lo:D_FF + lo + FF_CHUNK] = su_ref[slot].astype(BF16)
                    wo_ref[lo:lo + FF_CHUNK, :] = so_ref[slot].astype(BF16)
                g = jnp.dot(h, wi_ref[:, lo:lo + FF_CHUNK], preferred_element_type=F32)
                u = jnp.dot(h, wi_ref[:, D_FF + lo:D_FF + lo + FF_CHUNK], preferred_element_type=F32)
                a = (_silu(g) * u).astype(BF16)
                part = jnp.dot(a, wo_ref[lo:lo + FF_CHUNK, :], preferred_element_type=F32)
                if c == 0:
                    acc_ref[rows, :] = part
                else:
                    acc_ref[rows, :] += part
            out = x + (FFN_RES * gt) * _rms(acc_ref[rows, :], gpost_ref[0, 0])
            if split_out:
                @pl.when(is_ctx)
                def _(rows=rows, out=out):
                    o_refs[0][rows, :] = out

                @pl.when(jnp.logical_not(is_ctx))
                def _(rows=rows, out=out):
                    o_refs[1][rows, :] = out
            else:
                o_refs[0][rows, :] = out

    pl.when(i == 0)(functools.partial(run, True))
    pl.when(i != 0)(functools.partial(run, False))


def _ffn(xs, mod, norm_g, w_ffn_in, w_ffn_out, l, f, split_out=False):
    tm = TM_FFN
    ctx_tiles = T_CTX // tm
    split_in = isinstance(xs, tuple)

    def lf_block(shape, k):
        return pl.BlockSpec((1, 1) + shape, lambda i: (l, k, 0, 0), pipeline_mode=pl.Buffered(1))

    merged = pl.BlockSpec((tm, D_MODEL), lambda i: (i, 0))
    ctx_part = pl.BlockSpec((tm, D_MODEL), lambda i: (jnp.minimum(i, ctx_tiles - 1), 0))
    lat_part = pl.BlockSpec((tm, D_MODEL), lambda i: (jnp.maximum(i - ctx_tiles, 0), 0))
    parts_shape = [jax.ShapeDtypeStruct((T_CTX, D_MODEL), F32), jax.ShapeDtypeStruct((T_LAT, D_MODEL), F32)]
    return pl.pallas_call(
        functools.partial(_ffn_kernel, tm=tm, k0=6 * f, l=l, f=f, split_in=split_in, split_out=split_out),
        grid=(T_ALL // tm,),
        in_specs=([ctx_part, lat_part] if split_in else [merged]) + [
            _layer_block(l, (8, N_MOD * D_MODEL)),
            lf_block((1, D_MODEL), 4 * f),
            lf_block((1, D_MODEL), 4 * f + 1),
            pl.BlockSpec(memory_space=pl.ANY),
            pl.BlockSpec(memory_space=pl.ANY),
        ],
        out_specs=[ctx_part, lat_part] if split_out else merged,
        out_shape=parts_shape if split_out else jax.ShapeDtypeStruct((T_ALL, D_MODEL), F32),
        scratch_shapes=[pltpu.VMEM((tm, D_MODEL), F32),
                        pltpu.VMEM((D_MODEL, 2 * D_FF), BF16),
                        pltpu.VMEM((D_FF, D_MODEL), BF16),
                        pltpu.VMEM((2, D_MODEL, FF_CHUNK), F32),
                        pltpu.VMEM((2, D_MODEL, FF_CHUNK), F32),
                        pltpu.VMEM((2, FF_CHUNK, D_MODEL), F32),
                        pltpu.SemaphoreType.DMA((3, 2))],
        compiler_params=_cparams(("arbitrary",)),
        name="ffn",
    )(*(xs if split_in else (xs,)), mod, norm_g, norm_g, w_ffn_in, w_ffn_out)


def _proj_kernel(x_ref, xp_ref, xn_ref, mod_ref, g_ref, w_ref, wdt_ref, wdt_t_ref, dtb_row_ref, dtb_col_ref,
                 scw_ref, scb_ref,
                 z_ref, xs_ref, bc_ref, v_ref, pool_ref, gate_ref, dt_ref, dt_t_ref,
                 wa_ref, wb_ref, stage_ref, sem, xpad_ref, *, l):
    i = pl.program_id(0)
    lat = i >= CTX_TILES
    k = jnp.maximum(i - CTX_TILES, 0) % LAT_TILES
    has_prev = jnp.logical_and(lat, k != 0)
    has_next = jnp.logical_and(lat, k != LAT_TILES - 1)
    row = _mod_row(i, TILE)
    sh = _mod_vec(mod_ref, row, 3)
    sc = _mod_vec(mod_ref, row, 4)
    x_all = jnp.concatenate([x_ref[...], xp_ref[...], xn_ref[...]], axis=0)
    h_all = (_rms(x_all, g_ref[0, 0]) * (1.0 + sc) + sh).astype(BF16)
    h = h_all[:TILE]

    def slab_copy(r):
        return pltpu.make_async_copy(w_ref.at[l, pl.ds(r * W_SLAB, W_SLAB), :], stage_ref.at[r % 2], sem.at[r % 2])

    @pl.when(i == 0)
    def _():
        slab_copy(0).start()
        for r in range(D_MODEL // W_SLAB):
            if r + 1 < D_MODEL // W_SLAB:
                slab_copy(r + 1).start()
            slab_copy(r).wait()
            rows = slice(r * W_SLAB, (r + 1) * W_SLAB)
            wa_ref[rows, :] = stage_ref[r % 2, :, 0:N_WA].astype(BF16)
            wb_ref[rows, :] = stage_ref[r % 2, :, OFF_DT:IN_COLS].astype(BF16)

    def mm(w, lo, hi):
        return jnp.dot(h, w[:, lo:hi], preferred_element_type=F32)

    wa = wa_ref
    xbc = jnp.dot(h_all, wa[:, SSD_D_INNER:N_WA], preferred_element_type=F32)
    xpad_ref[HALO:HALO + TILE, :] = xbc[:TILE]
    xpad_ref[0:HALO, :] = jnp.where(has_prev, xbc[TILE:TILE + HALO], 0.0)
    xpad_ref[HALO + TILE:, :] = jnp.where(has_next, xbc[TILE + HALO:], 0.0)
    def conv_blocks(lo, hi):
        for cb in range(lo, hi):
            col = slice(cb * 128, (cb + 1) * 128)
            for r0 in range(0, TILE, CONV_ROWS):
                first = HALO - SSD_CONV // 2 + r0
                acc = scb_ref[0, :, col]
                for t in range(SSD_CONV):
                    acc = acc + scw_ref[0, t:t + 1, col] * xpad_ref[first + t:first + t + CONV_ROWS, col]
                if cb < SSD_D_INNER // 128:
                    xs_ref[r0:r0 + CONV_ROWS, col] = _silu(acc)
                else:
                    bc_ref[r0:r0 + CONV_ROWS, cb * 128 - SSD_D_INNER:(cb + 1) * 128 - SSD_D_INNER] = (
                        _silu(acc).astype(BF16))

    z_ref[...] = _silu(mm(wa, 0, SSD_D_INNER))
    conv_blocks(0, 4)
    ag = mm(wb_ref, 0, PB_POOL)
    v_ref[...] = ag[:, :CONF_D] * _sigmoid(ag[:, CONF_D:])
    conv_blocks(4, 7)
    pool_ref[...] = mm(wb_ref, PB_POOL, PB_GATE)
    for j in range(3):
        conv_blocks(7 + 3 * j, 10 + 3 * j)
        gate_ref[:, j * D_MODEL:(j + 1) * D_MODEL] = _sigmoid(
            mm(wb_ref, PB_GATE + j * D_MODEL, PB_GATE + (j + 1) * D_MODEL))
    dt_ref[...] = _softplus(jnp.dot(h, wdt_ref[0], preferred_element_type=F32) + dtb_row_ref[0])
    dt_t = lax.dot_general(wdt_t_ref[0], h, (((1,), (1,)), ((), ())), preferred_element_type=F32)
    dt_t_ref[...] = _softplus(dt_t + dtb_col_ref[0])


def _proj(x, mod, norm_g, w_b, wdt, wdt_t, dtb_row, dtb_col, scw, scb, l):
    per_tile = TILE // HALO
    n_halo = T_ALL // HALO

    def tile(n):
        return pl.BlockSpec((TILE, n), lambda i: (i, 0))

    outs = ((SSD_D_INNER, F32), (SSD_D_INNER, F32), (BC_COLS, BF16), (CONF_D, F32), (POOL_D, F32),
            (3 * D_MODEL, F32), (2 * DT_LANES, F32))
    return pl.pallas_call(
        functools.partial(_proj_kernel, l=l),
        grid=(N_TILES,),
        in_specs=[
            tile(D_MODEL),
            pl.BlockSpec((HALO, D_MODEL), lambda i: (jnp.maximum(i * per_tile - 1, 0), 0)),
            pl.BlockSpec((HALO, D_MODEL), lambda i: (jnp.minimum((i + 1) * per_tile, n_halo - 1), 0)),
            _layer_block(l, (8, N_MOD * D_MODEL)),
            pl.BlockSpec((1, 1, 1, D_MODEL), lambda i: (l, 2, 0, 0), pipeline_mode=pl.Buffered(1)),
            pl.BlockSpec(memory_space=pl.ANY),
            _layer_block(l, (D_MODEL, 2 * DT_LANES)),
            _layer_block(l, (2 * SSD_HEADS, D_MODEL)),
            _layer_block(l, (1, 2 * DT_LANES)),
            _layer_block(l, (2 * SSD_HEADS, 1)),
            _layer_block(l, (SSD_CONV, SSD_XBC)),
            _layer_block(l, (1, SSD_XBC)),
        ],
        out_specs=[tile(n) for n, _ in outs] + [pl.BlockSpec((2 * SSD_HEADS, TILE), lambda i: (0, i))],
        out_shape=[jax.ShapeDtypeStruct((T_ALL, n), dt) for n, dt in outs]
        + [jax.ShapeDtypeStruct((2 * SSD_HEADS, T_ALL), F32)],
        scratch_shapes=[pltpu.VMEM((D_MODEL, N_WA), BF16),
                        pltpu.VMEM((D_MODEL, N_WB), BF16),
                        pltpu.VMEM((2, W_SLAB, IN_COLS), F32),
                        pltpu.SemaphoreType.DMA((2,)),
                        pltpu.VMEM((TILE + 2 * HALO, SSD_XBC), F32)],
        compiler_params=_cparams(("arbitrary",)),
        name="in_proj",
    )(x, x, x, mod, norm_g, w_b, wdt, wdt_t, dtb_row, dtb_col, scw, scb)


_NT = (((1,), (1,)), ((), ()))


def _split3(v):
    p0 = v.astype(BF16)
    r = v - p0.astype(F32)
    p1 = r.astype(BF16)
    p2 = (r - p1.astype(F32)).astype(BF16)
    return p0, p1, p2


def _scan_kernel(*refs, nc, latent, first_layer):
    (xs_ref, bcs_ref, dt_ref, dtt_ref, tri_ref, alr_ref, alc_ref, dsk_ref) = refs[:8]
    refs = refs[8:]
    if latent:
        h0_ref, y_ref = refs[:2]
        refs = refs[2:]
    elif first_layer:
        y_ref, ns_ref = refs[:2]
        refs = refs[2:]
    else:
        y_ref, ns_ref = refs[1:3]
        refs = refs[3:]
    xt_ref, acs_ref, acst_ref, st_ref, s_ref, yt_ref = refs

    def rows_of(c):
        return pl.ds(pl.multiple_of(c * SSD_CHUNK, SSD_CHUNK), SSD_CHUNK)

    def dir_rows(d):
        return slice(d * SSD_HEADS, (d + 1) * SSD_HEADS)

    def b_cols(g):
        return slice(g * SSD_D_STATE, (g + 1) * SSD_D_STATE)

    def c_cols(g):
        lo = SSD_GROUPS * SSD_D_STATE + g * SSD_D_STATE
        return slice(lo, lo + SSD_D_STATE)

    def head_rows(h):
        return slice(h * SSD_HEAD_DIM, (h + 1) * SSD_HEAD_DIM)

    def total_col(acs_t, d):
        return acs_t[:, SSD_CHUNK - 1:] if d == 0 else acs_t[:, :1]

    def cumsums(c, carry):
        rows = rows_of(c)
        for d in range(2):
            tri = tri_ref[d]
            a_row = -jnp.exp(alr_ref[0, d])
            a_col = -jnp.exp(alc_ref[0, d])
            p = jnp.concatenate(_split3(dt_ref[rows, d * DT_LANES:(d + 1) * DT_LANES]), axis=1)
            r = jnp.dot(tri, p, preferred_element_type=F32)
            acs_ref[d, rows, :] = (r[:, :128] + r[:, 128:256] + r[:, 256:]) * a_row
            q = jnp.concatenate(_split3(dtt_ref[dir_rows(d), rows]), axis=0)
            rt = lax.dot_general(q, tri, _NT, preferred_element_type=F32)
            acst_ref[d, :, rows] = (rt[:SSD_HEADS] + rt[SSD_HEADS:2 * SSD_HEADS] + rt[2 * SSD_HEADS:]) * a_col
        return carry

    lax.fori_loop(0, nc, cumsums, 0)

    def local_states(c, carry):
        rows = rows_of(c)
        xt = xs_ref[rows, :].T
        xt_ref[:, rows] = xt
        for d in range(2):
            acs_t = acst_ref[d, :, rows]
            w = dtt_ref[dir_rows(d), rows] * jnp.exp(total_col(acs_t, d) - acs_t)
            for g in range(SSD_GROUPS):
                bg = bcs_ref[rows, b_cols(g)]
                parts = [(xt[head_rows(h), :] * w[h:h + 1, :]).astype(BF16)
                         for h in range(g * HEADS_PER_GROUP, (g + 1) * HEADS_PER_GROUP)]
                st_ref[d, c, g * GROUP_ROWS:(g + 1) * GROUP_ROWS, :] = jnp.dot(
                    jnp.concatenate(parts, axis=0), bg, preferred_element_type=F32)
        return carry

    lax.fori_loop(0, nc, local_states, 0)

    for d in range(2):
        if latent:
            s_ref[...] = h0_ref[0, 0, d]
        else:
            s_ref[...] = jnp.zeros(s_ref.shape, F32)

        def recur(j, carry, d=d):
            c = j if d == 0 else nc - 1 - j
            acs_t = acst_ref[d, :, rows_of(c)]
            e_tot = jnp.broadcast_to(jnp.exp(total_col(acs_t, d)), (SSD_HEADS, SSD_D_STATE))
            for h in range(SSD_HEADS):
                s_old = s_ref[head_rows(h), :]
                cs = st_ref[d, c, head_rows(h), :]
                st_ref[d, c, head_rows(h), :] = s_old
                s_ref[head_rows(h), :] = s_old * e_tot[h:h + 1, :] + cs
            return carry

        lax.fori_loop(0, nc, recur, 0)
        if not latent:
            ns_ref[0, 0, d] = s_ref[...]
    if first_layer and not latent:
        ns_ref[0, 1:] = jnp.zeros((DEPTH - 1, 2, SSD_D_INNER, SSD_D_STATE), F32)

    s_idx = lax.broadcasted_iota(jnp.int32, (SSD_CHUNK, SSD_CHUNK), 0)
    l_idx = lax.broadcasted_iota(jnp.int32, (SSD_CHUNK, SSD_CHUNK), 1)
    visible = (s_idx <= l_idx, s_idx >= l_idx)

    def outputs(c, carry):
        rows = rows_of(c)
        acs_t = [acst_ref[d, :, rows] for d in range(2)]
        e_acs_t = [jnp.exp(a) for a in acs_t]
        dt_t = [dtt_ref[dir_rows(d), rows] for d in range(2)]
        for g in range(SSD_GROUPS):
            bg = bcs_ref[rows, b_cols(g)]
            cg = bcs_ref[rows, c_cols(g)]
            g_t = lax.dot_general(bg, cg, _NT, preferred_element_type=F32)
            y_in = [lax.dot_general(st_ref[d, c, g * GROUP_ROWS:(g + 1) * GROUP_ROWS, :].astype(BF16), cg, _NT,
                                    preferred_element_type=F32) for d in range(2)]
            for hg in range(HEADS_PER_GROUP):
                h = g * HEADS_PER_GROUP + hg
                x_h = xt_ref[head_rows(h), rows]
                lhs, rhs = [], []
                y_h = dsk_ref[0, head_rows(h), :] * x_h
                for d in range(2):
                    seg = jnp.where(visible[d], acs_t[d][h:h + 1, :] - acs_ref[d, rows, h:h + 1], -jnp.inf)
                    rhs.append((g_t * jnp.exp(seg)).astype(BF16))
                    lhs.append((x_h * dt_t[d][h:h + 1, :]).astype(BF16))
                    y_h = y_h + y_in[d][hg * SSD_HEAD_DIM:(hg + 1) * SSD_HEAD_DIM, :] * e_acs_t[d][h:h + 1, :]
                y_h = y_h + jnp.dot(jnp.concatenate(lhs, axis=1), jnp.concatenate(rhs, axis=0),
                                    preferred_element_type=F32)
                yt_ref[head_rows(h), :] = y_h
        y_ref[rows, :] = yt_ref[...].T
        return carry

    lax.fori_loop(0, nc, outputs, 0)


def _scan(xs, bc, dt, dt_t, tri, alr, alc, dskip, state_ssd5, ns_all, l, latent):
    seq = DEC_SEQ if latent else SEQ
    n_seq = DEC_BATCH if latent else BATCH
    first = T_CTX // seq if latent else 0
    nc = seq // SSD_CHUNK
    first_layer = ns_all is None
    in_specs = [
        pl.BlockSpec((seq, SSD_D_INNER), lambda i: (first + i, 0)),
        pl.BlockSpec((seq, BC_COLS), lambda i: (first + i, 0)),
        pl.BlockSpec((seq, 2 * DT_LANES), lambda i: (first + i, 0)),
        pl.BlockSpec((2 * SSD_HEADS, seq), lambda i: (0, first + i)),
        pl.BlockSpec((2, SSD_CHUNK, SSD_CHUNK), lambda i: (0, 0, 0), pipeline_mode=pl.Buffered(1)),
        _layer_block(l, (2, 1, DT_LANES)),
        _layer_block(l, (2, SSD_HEADS, 1)),
        _layer_block(l, (SSD_D_INNER, SSD_D_STATE)),
    ]
    args = (xs, bc, dt, dt_t, tri, alr, alc, dskip)
    scratch = [pltpu.VMEM((SSD_D_INNER, seq), F32),
               pltpu.VMEM((2, seq, DT_LANES), F32),
               pltpu.VMEM((2, SSD_HEADS, seq), F32),
               pltpu.VMEM((2, nc, SSD_D_INNER, SSD_D_STATE), F32),
               pltpu.VMEM((SSD_D_INNER, SSD_D_STATE), F32),
               pltpu.VMEM((SSD_D_INNER, SSD_CHUNK), F32)]
    kern = functools.partial(_scan_kernel, nc=nc, latent=latent, first_layer=first_layer)
    y_spec = pl.BlockSpec((seq, SSD_D_INNER), lambda i: (i, 0))
    if latent:
        state_block = (1, 1, 2, SSD_D_INNER, SSD_D_STATE)
        return pl.pallas_call(
            kern, grid=(n_seq,),
            in_specs=in_specs + [pl.BlockSpec(state_block, lambda i: (i, l, 0, 0, 0))],
            out_specs=y_spec,
            out_shape=jax.ShapeDtypeStruct((T_LAT, SSD_D_INNER), F32),
            scratch_shapes=scratch,
            compiler_params=_cparams(("arbitrary",)),
            name="ssd_scan_latent",
        )(*args, state_ssd5)
    ns_shape = jax.ShapeDtypeStruct((BATCH, DEPTH, 2, SSD_D_INNER, SSD_D_STATE), F32)
    y_shape = jax.ShapeDtypeStruct((T_CTX, SSD_D_INNER), F32)
    if first_layer:
        return pl.pallas_call(
            kern, grid=(n_seq,),
            in_specs=in_specs,
            out_specs=[y_spec, pl.BlockSpec((1, DEPTH, 2, SSD_D_INNER, SSD_D_STATE), lambda i: (i, 0, 0, 0, 0))],
            out_shape=[y_shape, ns_shape],
            scratch_shapes=scratch,
            compiler_params=_cparams(("arbitrary",)),
            name="ssd_scan_ctx",
        )(*args)
    return pl.pallas_call(
        kern, grid=(n_seq,),
        in_specs=in_specs + [pl.BlockSpec(memory_space=pl.ANY)],
        out_specs=[y_spec, pl.BlockSpec((1, 1, 2, SSD_D_INNER, SSD_D_STATE), lambda i: (i, l, 0, 0, 0))],
        out_shape=[y_shape, ns_shape],
        scratch_shapes=scratch,
        input_output_aliases={len(args): 1},
        compiler_params=_cparams(("arbitrary",)),
        name="ssd_scan_ctx",
    )(*args, ns_all)


CONF_SHIFT_ROWS = TILE + 3 * 8
CONF_ROWS = 64


def _fill_padded(pad_ref, cur_ref, prev_ref, next_ref, has_prev, has_next):
    pad_ref[0:HALO, :] = jnp.where(has_prev, prev_ref[...], 0.0)
    pad_ref[HALO:HALO + TILE, :] = cur_ref[...]
    pad_ref[HALO + TILE:, :] = jnp.where(has_next, next_ref[...], 0.0)


def _merge_kernel(x_ref, mod_ref, yc_ref, yl_ref, z_ref, gate_ref,
                  vc_ref, vp_ref, vn_ref, pc_ref, pp_ref, pn_ref,
                  ccw_ref, ccb_ref, lng_ref, lnb_ref, wbc_ref, pw_ref, psc_ref, wbp_ref,
                  sng_ref, wbs_ref, wout_ref, gpost_ref,
                  o_ref,
                  vpad_ref, vsh_ref, ppad_ref, cv_ref, mix_ref):
    i = pl.program_id(0)
    lat = i >= CTX_TILES
    k = jnp.maximum(i - CTX_TILES, 0) % LAT_TILES
    has_prev = jnp.logical_and(lat, k != 0)
    has_next = jnp.logical_and(lat, k != LAT_TILES - 1)
    _fill_padded(vpad_ref, vc_ref, vp_ref, vn_ref, has_prev, has_next)
    _fill_padded(ppad_ref, pc_ref, pp_ref, pn_ref, has_prev, has_next)

    y = jnp.where(lat, yl_ref[...], yc_ref[...])
    br_ssd = jnp.dot(_rms(y * z_ref[...], sng_ref[0]).astype(BF16), wbs_ref[0], preferred_element_type=F32)

    for s in range(1, 8):
        vsh_ref[s - 1] = vpad_ref[s:s + CONF_SHIFT_ROWS, :]

    for cb in range(CONF_D // 128):
        col = slice(cb * 128, (cb + 1) * 128)
        first = HALO - CONF_KERNEL // 2
        for r0 in range(0, TILE, CONF_ROWS):
            acc = ccb_ref[0, :, col]
            for t in range(CONF_KERNEL):
                off = first + t
                a, s = off // 8, off % 8
                src = vpad_ref if s == 0 else vsh_ref.at[s - 1]
                acc = acc + ccw_ref[0, t:t + 1, col] * src[8 * a + r0:8 * a + r0 + CONF_ROWS, col]
            cv_ref[r0:r0 + CONF_ROWS, col] = acc
    cv = cv_ref[...]
    mu = jnp.mean(cv, axis=-1, keepdims=True)
    cen = cv - mu
    var = jnp.mean(cen * cen, axis=-1, keepdims=True)
    ln = cen * lax.rsqrt(var + EPS) * lng_ref[0] + lnb_ref[0]
    br_conf = jnp.dot(_silu(ln).astype(BF16), wbc_ref[0], preferred_element_type=F32)

    seq_len = jnp.where(lat, DEC_SEQ, SEQ)
    pos = k * TILE + lax.broadcasted_iota(jnp.int32, (TILE, 1), 0)
    for gi, w in enumerate(POOL_WINDOWS):
        cols = slice(gi * 128, (gi + 1) * 128)
        first = HALO - w // 2
        s = ppad_ref[first:first + TILE, cols]
        for j in range(1, w):
            s = s + ppad_ref[first + j:first + j + TILE, cols]
        lo = jnp.maximum(pos - w // 2, 0)
        hi = jnp.minimum(pos - w // 2 + w, seq_len)
        pooled = s / (hi - lo).astype(F32) - pc_ref[:, cols]
        mixed = jnp.dot(pooled.astype(BF16), pw_ref[0, gi], preferred_element_type=F32)
        mix_ref[:, cols] = mixed * psc_ref[0, :, cols]
    br_pool = jnp.dot(mix_ref[...].astype(BF16), wbp_ref[0], preferred_element_type=F32)

    merged =(gate_ref[:, 0:D_MODEL] * br_ssd
              + gate_ref[:, D_MODEL:2 * D_MODEL] * br_conf
              + gate_ref[:, 2 * D_MODEL:] * br_pool)
    yo = jnp.dot(merged.astype(BF16), wout_ref[0], preferred_element_type=F32)
    gt = _mod_vec(mod_ref, _mod_row(i, TILE), 5)
    o_ref[...] = x_ref[...] + gt * _rms(yo, gpost_ref[0, 0])


def _merge(x, mod, y_ctx, y_lat, z, gate, v, pool, ccw, ccb, lng, lnb, wbc, pw, psc, wbp, sng, wbs, wout,
           norm_g, l):
    per_tile = TILE // HALO
    n_halo = T_ALL // HALO

    def cur(n):
        return pl.BlockSpec((TILE, n), lambda i: (i, 0))

    def trio(n):
        return [cur(n),
                pl.BlockSpec((HALO, n), lambda i: (jnp.maximum(i * per_tile - 1, 0), 0)),
                pl.BlockSpec((HALO, n), lambda i: (jnp.minimum((i + 1) * per_tile, n_halo - 1), 0))]

    return pl.pallas_call(
        _merge_kernel,
        grid=(N_TILES,),
        in_specs=[
            cur(D_MODEL),
            _layer_block(l, (8, N_MOD * D_MODEL)),
            pl.BlockSpec((TILE, SSD_D_INNER), lambda i: (jnp.minimum(i, CTX_TILES - 1), 0)),
            pl.BlockSpec((TILE, SSD_D_INNER), lambda i: (jnp.maximum(i - CTX_TILES, 0), 0)),
            cur(SSD_D_INNER), cur(3 * D_MODEL)]
        + trio(CONF_D) + trio(POOL_D) + [
            _layer_block(l, (CONF_KERNEL, CONF_D)), _layer_block(l, (1, CONF_D)),
            _layer_block(l, (1, CONF_D)), _layer_block(l, (1, CONF_D)),
            _layer_block(l, (CONF_D, D_MODEL)),
            _layer_block(l, (len(POOL_WINDOWS), 128, 128)), _layer_block(l, (1, POOL_D)),
            _layer_block(l, (POOL_D, D_MODEL)),
            _layer_block(l, (1, SSD_D_INNER)),
            _layer_block(l, (SSD_D_INNER, D_MODEL)),
            _layer_block(l, (D_MODEL, D_MODEL)),
            pl.BlockSpec((1, 1, 1, D_MODEL), lambda i: (l, 3, 0, 0), pipeline_mode=pl.Buffered(1)),
        ],
        out_specs=cur(D_MODEL),
        out_shape=jax.ShapeDtypeStruct((T_ALL, D_MODEL), F32),
        scratch_shapes=[pltpu.VMEM((TILE + 2 * HALO, CONF_D), F32),
                        pltpu.VMEM((7, CONF_SHIFT_ROWS, CONF_D), F32),
                        pltpu.VMEM((TILE + 2 * HALO, POOL_D), F32),
                        pltpu.VMEM((TILE, CONF_D), F32),
                        pltpu.VMEM((TILE, POOL_D), F32)],
        compiler_params=_cparams(("arbitrary",)),
        name="mixer_merge",
    )(x, mod, y_ctx, y_lat, z, gate, v, v, v, pool, pool, pool,
      ccw, ccb, lng, lnb, wbc, pw, psc, wbp, sng, wbs, wout, norm_g)


def _pos_embed_2d(n_tokens):
    rows = n_tokens // GRID_W
    r, col = jnp.meshgrid(jnp.arange(rows), jnp.arange(GRID_W), indexing='ij')
    r = r.reshape(-1).astype(F32)
    col = col.reshape(-1).astype(F32)
    q = D_MODEL // 4
    omega = 1.0 / (10000.0 ** (jnp.arange(q, dtype=F32) / q))
    ar = r[:, None] * omega
    ac = col[:, None] * omega
    return jnp.concatenate([jnp.sin(ar), jnp.cos(ar), jnp.sin(ac), jnp.cos(ac)], axis=-1)


def _pad_lanes(a, n):
    return jnp.pad(a, [(0, 0)] * (a.ndim - 1) + [(0, n - a.shape[-1])])


def kernel(x_prompt, x_sample, state_ssd, c, c_ctx, w_mod, b_mod, norm_g, w_ffn_in, w_ffn_out, w_in,
           ssd_conv_w, ssd_conv_b, ssd_a_log, ssd_dt_bias, ssd_d, ssd_norm_g, w_br_ssd, conf_conv_w,
           conf_conv_b, conf_ln_g, conf_ln_b, w_br_conf, pool_w, pool_scale, w_br_pool, w_out):
    xs = x_sample + _pos_embed_2d(DEC_SEQ).astype(x_sample.dtype)[None]
    x = (x_prompt.reshape(T_CTX, D_MODEL), xs.reshape(T_LAT, D_MODEL))

    cond_t = jnp.concatenate([c_ctx[None, :], c, jnp.zeros((8 - 1 - DEC_BATCH, D_MODEL), F32)], axis=0).T
    mod = _modulation(cond_t, w_mod, b_mod)

    w_dt = lax.optimization_barrier(w_in[:, :, OFF_XBC:OFF_DT])
    wdt_b = jnp.concatenate([_pad_lanes(w_dt[:, :, :SSD_HEADS], DT_LANES),
                             _pad_lanes(w_dt[:, :, SSD_HEADS:], DT_LANES)], axis=-1).astype(BF16)
    wdt_t_b = jnp.swapaxes(w_dt, 1, 2).astype(BF16)
    dtb_row = _pad_lanes(ssd_dt_bias, DT_LANES).reshape(DEPTH, 1, 2 * DT_LANES)
    dtb_col = ssd_dt_bias.reshape(DEPTH, 2 * SSD_HEADS, 1)
    wbs_b = w_br_ssd.astype(BF16)
    wbc_b = w_br_conf.astype(BF16)
    wbp_b = w_br_pool.astype(BF16)
    wout_b = w_out.astype(BF16)
    pw_b = pool_w.astype(BF16)
    ng = norm_g.reshape(DEPTH, 6, 1, D_MODEL)
    alr = _pad_lanes(ssd_a_log, DT_LANES).reshape(DEPTH, 2, 1, DT_LANES)
    alc = ssd_a_log.reshape(DEPTH, 2, SSD_HEADS, 1)
    dskip = jnp.broadcast_to(jnp.repeat(ssd_d, SSD_HEAD_DIM, axis=1)[:, :, None],
                             (DEPTH, SSD_D_INNER, SSD_D_STATE))
    idx = jnp.arange(SSD_CHUNK)
    tri = jnp.stack([idx[:, None] >= idx[None, :], idx[:, None] <= idx[None, :]]).astype(BF16)
    state5 = state_ssd.reshape(DEC_BATCH, DEPTH, 2, SSD_D_INNER, SSD_D_STATE)
    ns_all = None

    def row1(a):
        return a.reshape(DEPTH, 1, a.shape[-1])

    for l in range(DEPTH):
        x = _ffn(x, mod, ng, w_ffn_in, w_ffn_out, l, 0)
        z, xs_c, bc_c, v, pool, gate, dt, dt_t = _proj(x, mod, ng, w_in, wdt_b, wdt_t_b, dtb_row, dtb_col,
                                                       ssd_conv_w, row1(ssd_conv_b), l)
        scan_args = (xs_c, bc_c, dt, dt_t, tri, alr, alc, dskip)
        y_ctx, ns_all = _scan(*scan_args, None, ns_all, l, latent=False)
        y_lat = _scan(*scan_args, state5, None, l, latent=True)
        x = _merge(x, mod, y_ctx, y_lat, z, gate, v, pool, conf_conv_w, row1(conf_conv_b),
                   row1(conf_ln_g), row1(conf_ln_b), wbc_b, pw_b, row1(pool_scale), wbp_b,
                   row1(ssd_norm_g), wbs_b, wout_b, ng, l)
        x = _ffn(x, mod, ng, w_ffn_in, w_ffn_out, l, 1, split_out=(l == DEPTH - 1))

    y_prompt = x[0].reshape(BATCH, SEQ, D_MODEL)
    y_sample = x[1].reshape(DEC_BATCH, DEC_SEQ, D_MODEL)
    new_state = ns_all.reshape(BATCH, DEPTH, 2, SSD_HEADS, SSD_HEAD_DIM, SSD_D_STATE).astype(x_prompt.dtype)
    return (y_prompt, y_sample, new_state)
```

```python
import functools

import jax
import jax.numpy as jnp
from jax import lax
from jax.experimental import pallas as pl
from jax.experimental.pallas import tpu as pltpu

F32 = jnp.float32
BF16 = jnp.bfloat16

D_MODEL = 1024
BATCH = 32
SEQ = 256
DEPTH = 4
DEC_BATCH = 2
DEC_SEQ = 1024
GRID_W = 64
SSD_D_INNER = 1024
SSD_HEAD_DIM = 64
SSD_HEADS = 16
SSD_GROUPS = 4
SSD_D_STATE = 128
SSD_CONV = 5
SSD_CHUNK = 128
SSD_XBC = 2048
CONF_D = 512
CONF_KERNEL = 31
POOL_D = 512
POOL_WINDOWS = (2, 4, 8, 16)
D_FF = 2816
N_MOD = 9
FFN_RES = 0.5
EPS = 1e-6
OFF_XBC = 3072
OFF_DT = 3104
IN_COLS = 7712

T_CTX = BATCH * SEQ
T_LAT = DEC_BATCH * DEC_SEQ
T_ALL = T_CTX + T_LAT
TILE = 256
HALO = 16
N_TILES = T_ALL // TILE
CTX_TILES = T_CTX // TILE
LAT_TILES = DEC_SEQ // TILE
HEADS_PER_GROUP = SSD_HEADS // SSD_GROUPS
GROUP_ROWS = HEADS_PER_GROUP * SSD_HEAD_DIM
BC_COLS = 2 * SSD_GROUPS * SSD_D_STATE

FF_CHUNK = 256
N_FF_CHUNKS = D_FF // FF_CHUNK
TM_FFN = 512
FFN_PARTS = 1

N_WA = OFF_XBC
N_WB = IN_COLS - OFF_DT
PB_POOL = 2 * CONF_D
PB_GATE = PB_POOL + POOL_D
DT_LANES = 128
CONV_ROWS = TILE
W_SLAB = 512

VMEM_LIMIT = 56 * 1024 * 1024


def _cparams(sem):
    return pltpu.CompilerParams(dimension_semantics=sem, vmem_limit_bytes=VMEM_LIMIT)


def _layer_block(l, shape):
    nd = len(shape)
    return pl.BlockSpec((1,) + tuple(shape), lambda *_: (l,) + (0,) * nd, pipeline_mode=pl.Buffered(1))


def _rms(x, g):
    ms = jnp.mean(x * x, axis=-1, keepdims=True)
    return x * lax.rsqrt(ms + EPS) * g


def _sigmoid(x):
    return 0.5 * jnp.tanh(0.5 * x) + 0.5


def _silu(x):
    u = 0.5 * x
    return u * jnp.tanh(u) + u


def _softplus(x):
    return jnp.maximum(x, 0.0) + jnp.log(1.0 + jnp.exp(-jnp.abs(x)))


def _mod_row(i, tm):
    ctx_tiles = T_CTX // tm
    per_seq = DEC_SEQ // tm
    return jnp.where(i < ctx_tiles, 0, 1 + jnp.maximum(i - ctx_tiles, 0) // per_seq)


def _mod_vec(mod_ref, row, k):
    return mod_ref[0, pl.ds(row, 1), pl.ds(k * D_MODEL, D_MODEL)]


def _mod_kernel(ct_ref, w_ref, b_ref, o_ref):
    ct = ct_ref[...]
    s = _silu(ct)
    w = w_ref[0]
    b = b_ref[0]
    o_ref[0] = jnp.zeros(o_ref.shape[1:], F32)
    for r in range(1 + DEC_BATCH):
        o_ref[0, r:r + 1, :] = jnp.sum(s[:, r:r + 1] * w, axis=0, keepdims=True) + b


def _modulation(cond_t, w_mod, b_mod):
    tn = 3072
    n_cols = N_MOD * D_MODEL
    return pl.pallas_call(
        _mod_kernel,
        grid=(DEPTH, n_cols // tn),
        in_specs=[
            pl.BlockSpec((D_MODEL, 8), lambda l, j: (0, 0)),
            pl.BlockSpec((1, D_MODEL, tn), lambda l, j: (l, 0, j)),
            pl.BlockSpec((1, 1, tn), lambda l, j: (l, 0, j)),
        ],
        out_specs=pl.BlockSpec((1, 8, tn), lambda l, j: (l, 0, j)),
        out_shape=jax.ShapeDtypeStruct((DEPTH, 8, n_cols), F32),
        compiler_params=_cparams(("arbitrary", "arbitrary")),
        name="modulation",
    )(cond_t, w_mod, b_mod.reshape(DEPTH, 1, n_cols))


def _ffn_kernel(*refs, tm, k0, l, f, split_in, split_out):
    refs = list(refs)
    x_refs = [refs.pop(0) for _ in range(2 if split_in else 1)]
    mod_ref, gpre_ref, gpost_ref, wi_hbm, wo_hbm = refs[:5]
    n_out = 2 if split_out else 1
    o_refs = refs[5:5 + n_out]
    acc_ref, wi_ref, wo_ref, sg_ref, su_ref, so_ref, sem = refs[5 + n_out:]
    i = pl.program_id(0)
    is_ctx = i < T_CTX // tm
    row = _mod_row(i, tm)
    sh = _mod_vec(mod_ref, row, k0)
    sc = _mod_vec(mod_ref, row, k0 + 1)
    gt = _mod_vec(mod_ref, row, k0 + 2)

    def load_x(rows):
        if split_in:
            return jnp.where(is_ctx, x_refs[0][rows, :], x_refs[1][rows, :])
        return x_refs[0][rows, :]

    def chunk_copies(c):
        lo, slot = c * FF_CHUNK, c % 2
        return (pltpu.make_async_copy(wi_hbm.at[l, f, :, pl.ds(lo, FF_CHUNK)], sg_ref.at[slot], sem.at[0, slot]),
                pltpu.make_async_copy(wi_hbm.at[l, f, :, pl.ds(D_FF + lo, FF_CHUNK)], su_ref.at[slot],
                                      sem.at[1, slot]),
                pltpu.make_async_copy(wo_hbm.at[l, f, pl.ds(lo, FF_CHUNK), :], so_ref.at[slot], sem.at[2, slot]))

    def run(stream_weights):
        if stream_weights:
            for cp in chunk_copies(0):
                cp.start()
        n_parts = 1 if stream_weights else FFN_PARTS
        pm = tm // n_parts
        for p in range(n_parts):
            rows = slice(p * pm, (p + 1) * pm)
            x = load_x(rows)
            h = (_rms(x, gpre_ref[0, 0]) * (1.0 + sc) + sh).astype(BF16)
            for c in range(N_FF_CHUNKS):
                lo = c * FF_CHUNK
                if stream_weights:
                    if c + 1 < N_FF_CHUNKS:
                        for cp in chunk_copies(c + 1):
                            cp.start()
                    for cp in chunk_copies(c):
                        cp.wait()
                    slot = c % 2
                    wi_ref[:, lo:lo + FF_CHUNK] = sg_ref[slot].astype(BF16)
                    wi_ref[:, D_FF + lo:D_FF + lo + FF_CHUNK] = su_ref[slot].astype(BF16)
                    wo_ref[lo:lo + FF_CHUNK, :] = so_ref[slot].astype(BF16)
                g = jnp.dot(h, wi_ref[:, lo:lo + FF_CHUNK], preferred_element_type=F32)
                u = jnp.dot(h, wi_ref[:, D_FF + lo:D_FF + lo + FF_CHUNK], preferred_element_type=F32)
                a = (_silu(g) * u).astype(BF16)
                part = jnp.dot(a, wo_ref[lo:lo + FF_CHUNK, :], preferred_element_type=F32)
                if c == 0:
                    acc_ref[rows, :] = part
                else:
                    acc_ref[rows, :] += part
            out = x + (FFN_RES * gt) * _rms(acc_ref[rows, :], gpost_ref[0, 0])
            if split_out:
                @pl.when(is_ctx)
                def _(rows=rows, out=out):
                    o_refs[0][rows, :] = out

                @pl.when(jnp.logical_not(is_ctx))
                def _(rows=rows, out=out):
                    o_refs[1][rows, :] = out
            else:
                o_refs[0][rows, :] = out

    pl.when(i == 0)(functools.partial(run, True))
    pl.when(i != 0)(functools.partial(run, False))


def _ffn(xs, mod, norm_g, w_ffn_in, w_ffn_out, l, f, split_out=False):
    tm = TM_FFN
    ctx_tiles = T_CTX // tm
    split_in = isinstance(xs, tuple)

    def lf_block(shape, k):
        return pl.BlockSpec((1, 1) + shape, lambda i: (l, k, 0, 0), pipeline_mode=pl.Buffered(1))

    merged = pl.BlockSpec((tm, D_MODEL), lambda i: (i, 0))
    ctx_part = pl.BlockSpec((tm, D_MODEL), lambda i: (jnp.minimum(i, ctx_tiles - 1), 0))
    lat_part = pl.BlockSpec((tm, D_MODEL), lambda i: (jnp.maximum(i - ctx_tiles, 0), 0))
    parts_shape = [jax.ShapeDtypeStruct((T_CTX, D_MODEL), F32), jax.ShapeDtypeStruct((T_LAT, D_MODEL), F32)]
    return pl.pallas_call(
        functools.partial(_ffn_kernel, tm=tm, k0=6 * f, l=l, f=f, split_in=split_in, split_out=split_out),
        grid=(T_ALL // tm,),
        in_specs=([ctx_part, lat_part] if split_in else [merged]) + [
            _layer_block(l, (8, N_MOD * D_MODEL)),
            lf_block((1, D_MODEL), 4 * f),
            lf_block((1, D_MODEL), 4 * f + 1),
            pl.BlockSpec(memory_space=pl.ANY),
            pl.BlockSpec(memory_space=pl.ANY),
        ],
        out_specs=[ctx_part, lat_part] if split_out else merged,
        out_shape=parts_shape if split_out else jax.ShapeDtypeStruct((T_ALL, D_MODEL), F32),
        scratch_shapes=[pltpu.VMEM((tm, D_MODEL), F32),
                        pltpu.VMEM((D_MODEL, 2 * D_FF), BF16),
                        pltpu.VMEM((D_FF, D_MODEL), BF16),
                        pltpu.VMEM((2, D_MODEL, FF_CHUNK), F32),
                        pltpu.VMEM((2, D_MODEL, FF_CHUNK), F32),
                        pltpu.VMEM((2, FF_CHUNK, D_MODEL), F32),
                        pltpu.SemaphoreType.DMA((3, 2))],
        compiler_params=_cparams(("arbitrary",)),
        name="ffn",
    )(*(xs if split_in else (xs,)), mod, norm_g, norm_g, w_ffn_in, w_ffn_out)


def _proj_kernel(x_ref, xp_ref, xn_ref, mod_ref, g_ref, w_ref, dtb_row_ref, dtb_col_ref,
                 scw_ref, scb_ref,
                 z_ref, xs_ref, bc_ref, v_ref, pool_ref, gate_ref, dt_ref, dt_t_ref,
                 wa_ref, wb_ref, wdt_ref, wdt_t_ref, stage_ref, sem, xpad_ref, *, l):
    i = pl.program_id(0)
    lat = i >= CTX_TILES
    k = jnp.maximum(i - CTX_TILES, 0) % LAT_TILES
    has_prev = jnp.logical_and(lat, k != 0)
    has_next = jnp.logical_and(lat, k != LAT_TILES - 1)
    row = _mod_row(i, TILE)
    sh = _mod_vec(mod_ref, row, 3)
    sc = _mod_vec(mod_ref, row, 4)
    x_all = jnp.concatenate([x_ref[...], xp_ref[...], xn_ref[...]], axis=0)
    h_all = (_rms(x_all, g_ref[0, 0]) * (1.0 + sc) + sh).astype(BF16)
    h = h_all[:TILE]

    slabs = ([(j * W_SLAB, W_SLAB, wa_ref, j * W_SLAB) for j in range(N_WA // W_SLAB)]
             + [(OFF_XBC, 2 * SSD_HEADS, None, 0)]
             + [(OFF_DT + j * W_SLAB, W_SLAB, wb_ref, j * W_SLAB) for j in range(N_WB // W_SLAB)])

    def slab_copy(s):
        src, n, _, _ = slabs[s]
        return pltpu.make_async_copy(w_ref.at[l, pl.ds(src, n), :], stage_ref.at[s % 2, pl.ds(0, n), :],
                                     sem.at[s % 2])

    @pl.when(i == 0)
    def _():
        slab_copy(0).start()
        for s, (_, n, dst_ref, dst) in enumerate(slabs):
            if s + 1 < len(slabs):
                slab_copy(s + 1).start()
            slab_copy(s).wait()
            if dst_ref is not None:
                dst_ref[dst:dst + n, :] = stage_ref[s % 2].astype(BF16)
            else:
                w_dt = stage_ref[s % 2, 0:2 * SSD_HEADS, :].astype(BF16)
                wdt_t_ref[...] = w_dt
                wdt_ref[...] = jnp.zeros(wdt_ref.shape, BF16)
                for d in range(2):
                    wdt_ref[d * DT_LANES:d * DT_LANES + SSD_HEADS, :] = w_dt[d * SSD_HEADS:(d + 1) * SSD_HEADS]

    def mm_t(a, w_rows):
        return lax.dot_general(a, w_rows, (((1,), (1,)), ((), ())), preferred_element_type=F32)

    def mm(w, lo, hi):
        return mm_t(h, w[lo:hi, :])

    wa = wa_ref
    xbc = mm_t(h_all, wa[SSD_D_INNER:N_WA, :])
    xpad_ref[HALO:HALO + TILE, :] = xbc[:TILE]
    xpad_ref[0:HALO, :] = jnp.where(has_prev, xbc[TILE:TILE + HALO], 0.0)
    xpad_ref[HALO + TILE:, :] = jnp.where(has_next, xbc[TILE + HALO:], 0.0)
    def conv_blocks(lo, hi):
        for cb in range(lo, hi):
            col = slice(cb * 128, (cb + 1) * 128)
            for r0 in range(0, TILE, CONV_ROWS):
                first = HALO - SSD_CONV // 2 + r0
                acc = scb_ref[0, :, col]
                for t in range(SSD_CONV):
                    acc = acc + scw_ref[0, t:t + 1, col] * xpad_ref[first + t:first + t + CONV_ROWS, col]
                if cb < SSD_D_INNER // 128:
                    xs_ref[r0:r0 + CONV_ROWS, col] = _silu(acc)
                else:
                    bc_ref[r0:r0 + CONV_ROWS, cb * 128 - SSD_D_INNER:(cb + 1) * 128 - SSD_D_INNER] = (
                        _silu(acc).astype(BF16))

    z_ref[...] = _silu(mm(wa, 0, SSD_D_INNER))
    conv_blocks(0, 4)
    ag = mm(wb_ref, 0, PB_POOL)
    v_ref[...] = ag[:, :CONF_D] * _sigmoid(ag[:, CONF_D:])
    conv_blocks(4, 7)
    pool_ref[...] = mm(wb_ref, PB_POOL, PB_GATE)
    for j in range(3):
        conv_blocks(7 + 3 * j, 10 + 3 * j)
        gate_ref[:, j * D_MODEL:(j + 1) * D_MODEL] = _sigmoid(
            mm(wb_ref, PB_GATE + j * D_MODEL, PB_GATE + (j + 1) * D_MODEL))
    dt_ref[...] = _softplus(mm_t(h, wdt_ref[...]) + dtb_row_ref[0])
    dt_t = lax.dot_general(wdt_t_ref[...], h, (((1,), (1,)), ((), ())), preferred_element_type=F32)
    dt_t_ref[...] = _softplus(dt_t + dtb_col_ref[0])


def _proj(x, mod, norm_g, w_in, dtb_row, dtb_col, scw, scb, l):
    per_tile = TILE // HALO
    n_halo = T_ALL // HALO

    def tile(n):
        return pl.BlockSpec((TILE, n), lambda i: (i, 0))

    outs = ((SSD_D_INNER, F32), (SSD_D_INNER, F32), (BC_COLS, BF16), (CONF_D, F32), (POOL_D, F32),
            (3 * D_MODEL, F32), (2 * DT_LANES, F32))
    return pl.pallas_call(
        functools.partial(_proj_kernel, l=l),
        grid=(N_TILES,),
        in_specs=[
            tile(D_MODEL),
            pl.BlockSpec((HALO, D_MODEL), lambda i: (jnp.maximum(i * per_tile - 1, 0), 0)),
            pl.BlockSpec((HALO, D_MODEL), lambda i: (jnp.minimum((i + 1) * per_tile, n_halo - 1), 0)),
            _layer_block(l, (8, N_MOD * D_MODEL)),
            pl.BlockSpec((1, 1, 1, D_MODEL), lambda i: (l, 2, 0, 0), pipeline_mode=pl.Buffered(1)),
            pl.BlockSpec(memory_space=pl.ANY),
            _layer_block(l, (1, 2 * DT_LANES)),
            _layer_block(l, (2 * SSD_HEADS, 1)),
            _layer_block(l, (SSD_CONV, SSD_XBC)),
            _layer_block(l, (1, SSD_XBC)),
        ],
        out_specs=[tile(n) for n, _ in outs] + [pl.BlockSpec((2 * SSD_HEADS, TILE), lambda i: (0, i))],
        out_shape=[jax.ShapeDtypeStruct((T_ALL, n), dt) for n, dt in outs]
        + [jax.ShapeDtypeStruct((2 * SSD_HEADS, T_ALL), F32)],
        scratch_shapes=[pltpu.VMEM((N_WA, D_MODEL), BF16),
                        pltpu.VMEM((N_WB, D_MODEL), BF16),
                        pltpu.VMEM((2 * DT_LANES, D_MODEL), BF16),
                        pltpu.VMEM((2 * SSD_HEADS, D_MODEL), BF16),
                        pltpu.VMEM((2, W_SLAB, D_MODEL), F32),
                        pltpu.SemaphoreType.DMA((2,)),
                        pltpu.VMEM((TILE + 2 * HALO, SSD_XBC), F32)],
        compiler_params=_cparams(("arbitrary",)),
        name="in_proj",
    )(x, x, x, mod, norm_g, w_in, dtb_row, dtb_col, scw, scb)


_NT = (((1,), (1,)), ((), ()))


def _split3(v):
    p0 = v.astype(BF16)
    r = v - p0.astype(F32)
    p1 = r.astype(BF16)
    p2 = (r - p1.astype(F32)).astype(BF16)
    return p0, p1, p2


def _scan_kernel(*refs, nc, latent, first_layer):
    (xs_ref, bcs_ref, dt_ref, dtt_ref, tri_ref, alr_ref, alc_ref, dsk_ref) = refs[:8]
    refs = refs[8:]
    if latent:
        h0_ref, y_ref = refs[:2]
        refs = refs[2:]
    elif first_layer:
        y_ref, ns_ref = refs[:2]
        refs = refs[2:]
    else:
        y_ref, ns_ref = refs[1:3]
        refs = refs[3:]
    xt_ref, acs_ref, acst_ref, st_ref, s_ref, yt_ref = refs

    def rows_of(c):
        return pl.ds(pl.multiple_of(c * SSD_CHUNK, SSD_CHUNK), SSD_CHUNK)

    def dir_rows(d):
        return slice(d * SSD_HEADS, (d + 1) * SSD_HEADS)

    def b_cols(g):
        return slice(g * SSD_D_STATE, (g + 1) * SSD_D_STATE)

    def c_cols(g):
        lo = SSD_GROUPS * SSD_D_STATE + g * SSD_D_STATE
        return slice(lo, lo + SSD_D_STATE)

    def head_rows(h):
        return slice(h * SSD_HEAD_DIM, (h + 1) * SSD_HEAD_DIM)

    def total_col(acs_t, d):
        return acs_t[:, SSD_CHUNK - 1:] if d == 0 else acs_t[:, :1]

    def cumsums(c, carry):
        rows = rows_of(c)
        for d in range(2):
            tri = tri_ref[d]
            a_row = -jnp.exp(alr_ref[0, d])
            a_col = -jnp.exp(alc_ref[0, d])
            p = jnp.concatenate(_split3(dt_ref[rows, d * DT_LANES:(d + 1) * DT_LANES]), axis=1)
            r = jnp.dot(tri, p, preferred_element_type=F32)
            acs_ref[d, rows, :] = (r[:, :128] + r[:, 128:256] + r[:, 256:]) * a_row
            q = jnp.concatenate(_split3(dtt_ref[dir_rows(d), rows]), axis=0)
            rt = lax.dot_general(q, tri, _NT, preferred_element_type=F32)
            acst_ref[d, :, rows] = (rt[:SSD_HEADS] + rt[SSD_HEADS:2 * SSD_HEADS] + rt[2 * SSD_HEADS:]) * a_col
        return carry

    lax.fori_loop(0, nc, cumsums, 0)

    def local_states(c, carry):
        rows = rows_of(c)
        xt = xs_ref[rows, :].T
        xt_ref[:, rows] = xt
        for d in range(2):
            acs_t = acst_ref[d, :, rows]
            w = dtt_ref[dir_rows(d), rows] * jnp.exp(total_col(acs_t, d) - acs_t)
            for g in range(SSD_GROUPS):
                bg = bcs_ref[rows, b_cols(g)]
                parts = [(xt[head_rows(h), :] * w[h:h + 1, :]).astype(BF16)
                         for h in range(g * HEADS_PER_GROUP, (g + 1) * HEADS_PER_GROUP)]
                st_ref[d, c, g * GROUP_ROWS:(g + 1) * GROUP_ROWS, :] = jnp.dot(
                    jnp.concatenate(parts, axis=0), bg, preferred_element_type=F32)
        return carry

    lax.fori_loop(0, nc, local_states, 0)

    for d in range(2):
        if latent:
            s_ref[...] = h0_ref[0, 0, d]
        else:
            s_ref[...] = jnp.zeros(s_ref.shape, F32)

        def recur(j, carry, d=d):
            c = j if d == 0 else nc - 1 - j
            acs_t = acst_ref[d, :, rows_of(c)]
            e_tot = jnp.broadcast_to(jnp.exp(total_col(acs_t, d)), (SSD_HEADS, SSD_D_STATE))
            for h in range(SSD_HEADS):
                s_old = s_ref[head_rows(h), :]
                cs = st_ref[d, c, head_rows(h), :]
                st_ref[d, c, head_rows(h), :] = s_old
                s_ref[head_rows(h), :] = s_old * e_tot[h:h + 1, :] + cs
            return carry

        lax.fori_loop(0, nc, recur, 0)
        if not latent:
            ns_ref[0, 0, d] = s_ref[...]
    if first_layer and not latent:
        ns_ref[0, 1:] = jnp.zeros((DEPTH - 1, 2, SSD_D_INNER, SSD_D_STATE), F32)

    s_idx = lax.broadcasted_iota(jnp.int32, (SSD_CHUNK, SSD_CHUNK), 0)
    l_idx = lax.broadcasted_iota(jnp.int32, (SSD_CHUNK, SSD_CHUNK), 1)
    visible = (s_idx <= l_idx, s_idx >= l_idx)

    def outputs(c, carry):
        rows = rows_of(c)
        acs_t = [acst_ref[d, :, rows] for d in range(2)]
        e_acs_t = [jnp.exp(a) for a in acs_t]
        dt_t = [dtt_ref[dir_rows(d), rows] for d in range(2)]
        for g in range(SSD_GROUPS):
            bg = bcs_ref[rows, b_cols(g)]
            cg = bcs_ref[rows, c_cols(g)]
            g_t = lax.dot_general(bg, cg, _NT, preferred_element_type=F32)
            y_in = [lax.dot_general(st_ref[d, c, g * GROUP_ROWS:(g + 1) * GROUP_ROWS, :].astype(BF16), cg, _NT,
                                    preferred_element_type=F32) for d in range(2)]
            for hg in range(HEADS_PER_GROUP):
                h = g * HEADS_PER_GROUP + hg
                x_h = xt_ref[head_rows(h), rows]
                lhs, rhs = [], []
                y_h = dsk_ref[0, head_rows(h), :] * x_h
                for d in range(2):
                    seg = jnp.where(visible[d], acs_t[d][h:h + 1, :] - acs_ref[d, rows, h:h + 1], -jnp.inf)
                    rhs.append((g_t * jnp.exp(seg)).astype(BF16))
                    lhs.append((x_h * dt_t[d][h:h + 1, :]).astype(BF16))
                    y_h = y_h + y_in[d][hg * SSD_HEAD_DIM:(hg + 1) * SSD_HEAD_DIM, :] * e_acs_t[d][h:h + 1, :]
                y_h = y_h + jnp.dot(jnp.concatenate(lhs, axis=1), jnp.concatenate(rhs, axis=0),
                                    preferred_element_type=F32)
                yt_ref[head_rows(h), :] = y_h
        y_ref[rows, :] = yt_ref[...].T
        return carry

    lax.fori_loop(0, nc, outputs, 0)


def _scan(xs, bc, dt, dt_t, tri, alr, alc, dskip, state_ssd5, ns_all, l, latent):
    seq = DEC_SEQ if latent else SEQ
    n_seq = DEC_BATCH if latent else BATCH
    first = T_CTX // seq if latent else 0
    nc = seq // SSD_CHUNK
    first_layer = ns_all is None
    in_specs = [
        pl.BlockSpec((seq, SSD_D_INNER), lambda i: (first + i, 0)),
        pl.BlockSpec((seq, BC_COLS), lambda i: (first + i, 0)),
        pl.BlockSpec((seq, 2 * DT_LANES), lambda i: (first + i, 0)),
        pl.BlockSpec((2 * SSD_HEADS, seq), lambda i: (0, first + i)),
        pl.BlockSpec((2, SSD_CHUNK, SSD_CHUNK), lambda i: (0, 0, 0), pipeline_mode=pl.Buffered(1)),
        _layer_block(l, (2, 1, DT_LANES)),
        _layer_block(l, (2, SSD_HEADS, 1)),
        _layer_block(l, (SSD_D_INNER, SSD_D_STATE)),
    ]
    args = (xs, bc, dt, dt_t, tri, alr, alc, dskip)
    scratch = [pltpu.VMEM((SSD_D_INNER, seq), F32),
               pltpu.VMEM((2, seq, DT_LANES), F32),
               pltpu.VMEM((2, SSD_HEADS, seq), F32),
               pltpu.VMEM((2, nc, SSD_D_INNER, SSD_D_STATE), F32),
               pltpu.VMEM((SSD_D_INNER, SSD_D_STATE), F32),
               pltpu.VMEM((SSD_D_INNER, SSD_CHUNK), F32)]
    kern = functools.partial(_scan_kernel, nc=nc, latent=latent, first_layer=first_layer)
    y_spec = pl.BlockSpec((seq, SSD_D_INNER), lambda i: (i, 0))
    if latent:
        state_block = (1, 1, 2, SSD_D_INNER, SSD_D_STATE)
        return pl.pallas_call(
            kern, grid=(n_seq,),
            in_specs=in_specs + [pl.BlockSpec(state_block, lambda i: (i, l, 0, 0, 0))],
            out_specs=y_spec,
            out_shape=jax.ShapeDtypeStruct((T_LAT, SSD_D_INNER), F32),
            scratch_shapes=scratch,
            compiler_params=_cparams(("arbitrary",)),
            name="ssd_scan_latent",
        )(*args, state_ssd5)
    ns_shape = jax.ShapeDtypeStruct((BATCH, DEPTH, 2, SSD_D_INNER, SSD_D_STATE), F32)
    y_shape = jax.ShapeDtypeStruct((T_CTX, SSD_D_INNER), F32)
    if first_layer:
        return pl.pallas_call(
            kern, grid=(n_seq,),
            in_specs=in_specs,
            out_specs=[y_spec, pl.BlockSpec((1, DEPTH, 2, SSD_D_INNER, SSD_D_STATE), lambda i: (i, 0, 0, 0, 0))],
            out_shape=[y_shape, ns_shape],
            scratch_shapes=scratch,
            compiler_params=_cparams(("arbitrary",)),
            name="ssd_scan_ctx",
        )(*args)
    return pl.pallas_call(
        kern, grid=(n_seq,),
        in_specs=in_specs + [pl.BlockSpec(memory_space=pl.ANY)],
        out_specs=[y_spec, pl.BlockSpec((1, 1, 2, SSD_D_INNER, SSD_D_STATE), lambda i: (i, l, 0, 0, 0))],
        out_shape=[y_shape, ns_shape],
        scratch_shapes=scratch,
        input_output_aliases={len(args): 1},
        compiler_params=_cparams(("arbitrary",)),
        name="ssd_scan_ctx",
    )(*args, ns_all)


CONF_SHIFT_ROWS = TILE + 3 * 8
CONF_ROWS = 64


def _fill_padded(pad_ref, cur_ref, prev_ref, next_ref, has_prev, has_next):
    pad_ref[0:HALO, :] = jnp.where(has_prev, prev_ref[...], 0.0)
    pad_ref[HALO:HALO + TILE, :] = cur_ref[...]
    pad_ref[HALO + TILE:, :] = jnp.where(has_next, next_ref[...], 0.0)


def _merge_kernel(x_ref, mod_ref, yc_ref, yl_ref, z_ref, gate_ref,
                  vc_ref, vp_ref, vn_ref, pc_ref, pp_ref, pn_ref,
                  ccw_ref, ccb_ref, lng_ref, lnb_ref, wbc_ref, pw_ref, psc_ref, wbp_ref,
                  sng_ref, wbs_ref, wout_ref, gpost_ref,
                  o_ref,
                  vpad_ref, vsh_ref, ppad_ref, cv_ref, mix_ref):
    i = pl.program_id(0)
    lat = i >= CTX_TILES
    k = jnp.maximum(i - CTX_TILES, 0) % LAT_TILES
    has_prev = jnp.logical_and(lat, k != 0)
    has_next = jnp.logical_and(lat, k != LAT_TILES - 1)
    _fill_padded(vpad_ref, vc_ref, vp_ref, vn_ref, has_prev, has_next)
    _fill_padded(ppad_ref, pc_ref, pp_ref, pn_ref, has_prev, has_next)

    y = jnp.where(lat, yl_ref[...], yc_ref[...])
    br_ssd = jnp.dot(_rms(y * z_ref[...], sng_ref[0]).astype(BF16), wbs_ref[0], preferred_element_type=F32)

    for s in range(1, 8):
        vsh_ref[s - 1] = vpad_ref[s:s + CONF_SHIFT_ROWS, :]

    for cb in range(CONF_D // 128):
        col = slice(cb * 128, (cb + 1) * 128)
        first = HALO - CONF_KERNEL // 2
        for r0 in range(0, TILE, CONF_ROWS):
            acc = ccb_ref[0, :, col]
            for t in range(CONF_KERNEL):
                off = first + t
                a, s = off // 8, off % 8
                src = vpad_ref if s == 0 else vsh_ref.at[s - 1]
                acc = acc + ccw_ref[0, t:t + 1, col] * src[8 * a + r0:8 * a + r0 + CONF_ROWS, col]
            cv_ref[r0:r0 + CONF_ROWS, col] = acc
    cv = cv_ref[...]
    mu = jnp.mean(cv, axis=-1, keepdims=True)
    cen = cv - mu
    var = jnp.mean(cen * cen, axis=-1, keepdims=True)
    ln = cen * lax.rsqrt(var + EPS) * lng_ref[0] + lnb_ref[0]
    br_conf = jnp.dot(_silu(ln).astype(BF16), wbc_ref[0], preferred_element_type=F32)

    seq_len = jnp.where(lat, DEC_SEQ, SEQ)
    pos = k * TILE + lax.broadcasted_iota(jnp.int32, (TILE, 1), 0)
    for gi, w in enumerate(POOL_WINDOWS):
        cols = slice(gi * 128, (gi + 1) * 128)
        first = HALO - w // 2
        s = ppad_ref[first:first + TILE, cols]
        for j in range(1, w):
            s = s + ppad_ref[first + j:first + j + TILE, cols]
        lo = jnp.maximum(pos - w // 2, 0)
        hi = jnp.minimum(pos - w // 2 + w, seq_len)
        pooled = s / (hi - lo).astype(F32) - pc_ref[:, cols]
        mixed = jnp.dot(pooled.astype(BF16), pw_ref[0, gi], preferred_element_type=F32)
        mix_ref[:, cols] = mixed * psc_ref[0, :, cols]
    br_pool = jnp.dot(mix_ref[...].astype(BF16), wbp_ref[0], preferred_element_type=F32)

    merged =(gate_ref[:, 0:D_MODEL] * br_ssd
              + gate_ref[:, D_MODEL:2 * D_MODEL] * br_conf
              + gate_ref[:, 2 * D_MODEL:] * br_pool)
    yo = jnp.dot(merged.astype(BF16), wout_ref[0], preferred_element_type=F32)
    gt = _mod_vec(mod_ref, _mod_row(i, TILE), 5)
    o_ref[...] = x_ref[...] + gt * _rms(yo, gpost_ref[0, 0])


def _merge(x, mod, y_ctx, y_lat, z, gate, v, pool, ccw, ccb, lng, lnb, wbc, pw, psc, wbp, sng, wbs, wout,
           norm_g, l):
    per_tile = TILE // HALO
    n_halo = T_ALL // HALO

    def cur(n):
        return pl.BlockSpec((TILE, n), lambda i: (i, 0))

    def trio(n):
        return [cur(n),
                pl.BlockSpec((HALO, n), lambda i: (jnp.maximum(i * per_tile - 1, 0), 0)),
                pl.BlockSpec((HALO, n), lambda i: (jnp.minimum((i + 1) * per_tile, n_halo - 1), 0))]

    return pl.pallas_call(
        _merge_kernel,
        grid=(N_TILES,),
        in_specs=[
            cur(D_MODEL),
            _layer_block(l, (8, N_MOD * D_MODEL)),
            pl.BlockSpec((TILE, SSD_D_INNER), lambda i: (jnp.minimum(i, CTX_TILES - 1), 0)),
            pl.BlockSpec((TILE, SSD_D_INNER), lambda i: (jnp.maximum(i - CTX_TILES, 0), 0)),
            cur(SSD_D_INNER), cur(3 * D_MODEL)]
        + trio(CONF_D) + trio(POOL_D) + [
            _layer_block(l, (CONF_KERNEL, CONF_D)), _layer_block(l, (1, CONF_D)),
            _layer_block(l, (1, CONF_D)), _layer_block(l, (1, CONF_D)),
            _layer_block(l, (CONF_D, D_MODEL)),
            _layer_block(l, (len(POOL_WINDOWS), 128, 128)), _layer_block(l, (1, POOL_D)),
            _layer_block(l, (POOL_D, D_MODEL)),
            _layer_block(l, (1, SSD_D_INNER)),
            _layer_block(l, (SSD_D_INNER, D_MODEL)),
            _layer_block(l, (D_MODEL, D_MODEL)),
            pl.BlockSpec((1, 1, 1, D_MODEL), lambda i: (l, 3, 0, 0), pipeline_mode=pl.Buffered(1)),
        ],
        out_specs=cur(D_MODEL),
        out_shape=jax.ShapeDtypeStruct((T_ALL, D_MODEL), F32),
        scratch_shapes=[pltpu.VMEM((TILE + 2 * HALO, CONF_D), F32),
                        pltpu.VMEM((7, CONF_SHIFT_ROWS, CONF_D), F32),
                        pltpu.VMEM((TILE + 2 * HALO, POOL_D), F32),
                        pltpu.VMEM((TILE, CONF_D), F32),
                        pltpu.VMEM((TILE, POOL_D), F32)],
        compiler_params=_cparams(("arbitrary",)),
        name="mixer_merge",
    )(x, mod, y_ctx, y_lat, z, gate, v, v, v, pool, pool, pool,
      ccw, ccb, lng, lnb, wbc, pw, psc, wbp, sng, wbs, wout, norm_g)


def _pos_embed_2d(n_tokens):
    rows = n_tokens // GRID_W
    r, col = jnp.meshgrid(jnp.arange(rows), jnp.arange(GRID_W), indexing='ij')
    r = r.reshape(-1).astype(F32)
    col = col.reshape(-1).astype(F32)
    q = D_MODEL // 4
    omega = 1.0 / (10000.0 ** (jnp.arange(q, dtype=F32) / q))
    ar = r[:, None] * omega
    ac = col[:, None] * omega
    return jnp.concatenate([jnp.sin(ar), jnp.cos(ar), jnp.sin(ac), jnp.cos(ac)], axis=-1)


def _pad_lanes(a, n):
    return jnp.pad(a, [(0, 0)] * (a.ndim - 1) + [(0, n - a.shape[-1])])


def kernel(x_prompt, x_sample, state_ssd, c, c_ctx, w_mod, b_mod, norm_g, w_ffn_in, w_ffn_out, w_in,
           ssd_conv_w, ssd_conv_b, ssd_a_log, ssd_dt_bias, ssd_d, ssd_norm_g, w_br_ssd, conf_conv_w,
           conf_conv_b, conf_ln_g, conf_ln_b, w_br_conf, pool_w, pool_scale, w_br_pool, w_out):
    xs = x_sample + _pos_embed_2d(DEC_SEQ).astype(x_sample.dtype)[None]
    x = (x_prompt.reshape(T_CTX, D_MODEL), xs.reshape(T_LAT, D_MODEL))

    cond_t = jnp.concatenate([c_ctx[None, :], c, jnp.zeros((8 - 1 - DEC_BATCH, D_MODEL), F32)], axis=0).T
    mod = _modulation(cond_t, w_mod, b_mod)

    w_in_t = jnp.swapaxes(w_in, 1, 2)
    dtb_row = _pad_lanes(ssd_dt_bias, DT_LANES).reshape(DEPTH, 1, 2 * DT_LANES)
    dtb_col = ssd_dt_bias.reshape(DEPTH, 2 * SSD_HEADS, 1)
    wbs_b = w_br_ssd.astype(BF16)
    wbc_b = w_br_conf.astype(BF16)
    wbp_b = w_br_pool.astype(BF16)
    wout_b = w_out.astype(BF16)
    pw_b = pool_w.astype(BF16)
    ng = norm_g.reshape(DEPTH, 6, 1, D_MODEL)
    alr = _pad_lanes(ssd_a_log, DT_LANES).reshape(DEPTH, 2, 1, DT_LANES)
    alc = ssd_a_log.reshape(DEPTH, 2, SSD_HEADS, 1)
    dskip = jnp.broadcast_to(jnp.repeat(ssd_d, SSD_HEAD_DIM, axis=1)[:, :, None],
                             (DEPTH, SSD_D_INNER, SSD_D_STATE))
    idx = jnp.arange(SSD_CHUNK)
    tri = jnp.stack([idx[:, None] >= idx[None, :], idx[:, None] <= idx[None, :]]).astype(BF16)
    state5 = state_ssd.reshape(DEC_BATCH, DEPTH, 2, SSD_D_INNER, SSD_D_STATE)
    ns_all = None

    def row1(a):
        return a.reshape(DEPTH, 1, a.shape[-1])

    for l in range(DEPTH):
        x = _ffn(x, mod, ng, w_ffn_in, w_ffn_out, l, 0)
        z, xs_c, bc_c, v, pool, gate, dt, dt_t = _proj(x, mod, ng, w_in_t, dtb_row, dtb_col,
                                                       ssd_conv_w, row1(ssd_conv_b), l)
        scan_args = (xs_c, bc_c, dt, dt_t, tri, alr, alc, dskip)
        y_ctx, ns_all = _scan(*scan_args, None, ns_all, l, latent=False)
        y_lat = _scan(*scan_args, state5, None, l, latent=True)
        x = _merge(x, mod, y_ctx, y_lat, z, gate, v, pool, conf_conv_w, row1(conf_conv_b),
                   row1(conf_ln_g), row1(conf_ln_b), wbc_b, pw_b, row1(pool_scale), wbp_b,
                   row1(ssd_norm_g), wbs_b, wout_b, ng, l)
        x = _ffn(x, mod, ng, w_ffn_in, w_ffn_out, l, 1, split_out=(l == DEPTH - 1))

    y_prompt = x[0].reshape(BATCH, SEQ, D_MODEL)
    y_sample = x[1].reshape(DEC_BATCH, DEC_SEQ, D_MODEL)
    new_state = ns_all.reshape(BATCH, DEPTH, 2, SSD_HEADS, SSD_HEAD_DIM, SSD_D_STATE).astype(x_prompt.dtype)
    return (y_prompt, y_sample, new_state)
```

```python
import functools

import jax
import jax.numpy as jnp
from jax import lax
from jax.experimental import pallas as pl
from jax.experimental.pallas import tpu as pltpu

F32 = jnp.float32
BF16 = jnp.bfloat16

D_MODEL = 1024
BATCH = 32
SEQ = 256
DEPTH = 4
DEC_BATCH = 2
DEC_SEQ = 1024
GRID_W = 64
SSD_D_INNER = 1024
SSD_HEAD_DIM = 64
SSD_HEADS = 16
SSD_GROUPS = 4
SSD_D_STATE = 128
SSD_CONV = 5
SSD_CHUNK = 128
SSD_XBC = 2048
CONF_D = 512
CONF_KERNEL = 31
POOL_D = 512
POOL_WINDOWS = (2, 4, 8, 16)
D_FF = 2816
N_MOD = 9
FFN_RES = 0.5
EPS = 1e-6
OFF_XBC = 3072
OFF_DT = 3104
IN_COLS = 7712

T_CTX = BATCH * SEQ
T_LAT = DEC_BATCH * DEC_SEQ
T_ALL = T_CTX + T_LAT
TILE = 256
HALO = 16
N_TILES = T_ALL // TILE
CTX_TILES = T_CTX // TILE
LAT_TILES = DEC_SEQ // TILE
HEADS_PER_GROUP = SSD_HEADS // SSD_GROUPS
GROUP_ROWS = HEADS_PER_GROUP * SSD_HEAD_DIM
BC_COLS = 2 * SSD_GROUPS * SSD_D_STATE

FF_CHUNK = 256
N_FF_CHUNKS = D_FF // FF_CHUNK
TM_FFN = 512
FFN_PARTS = 1

N_WA = OFF_XBC
N_WB = IN_COLS - OFF_DT
PB_POOL = 2 * CONF_D
PB_GATE = PB_POOL + POOL_D
DT_LANES = 128
CONV_ROWS = TILE
W_SLAB = 512

VMEM_LIMIT = 56 * 1024 * 1024


def _cparams(sem):
    return pltpu.CompilerParams(dimension_semantics=sem, vmem_limit_bytes=VMEM_LIMIT)


def _layer_block(l, shape):
    nd = len(shape)
    return pl.BlockSpec((1,) + tuple(shape), lambda *_: (l,) + (0,) * nd, pipeline_mode=pl.Buffered(1))


def _rms(x, g):
    ms = jnp.mean(x * x, axis=-1, keepdims=True)
    return x * lax.rsqrt(ms + EPS) * g


def _sigmoid(x):
    return 0.5 * jnp.tanh(0.5 * x) + 0.5


def _silu(x):
    u = 0.5 * x
    return u * jnp.tanh(u) + u


def _softplus(x):
    return jnp.maximum(x, 0.0) + jnp.log(1.0 + jnp.exp(-jnp.abs(x)))


def _mod_row(i, tm):
    ctx_tiles = T_CTX // tm
    per_seq = DEC_SEQ // tm
    return jnp.where(i < ctx_tiles, 0, 1 + jnp.maximum(i - ctx_tiles, 0) // per_seq)


def _mod_vec(mod_ref, row, k):
    return mod_ref[0, pl.ds(row, 1), pl.ds(k * D_MODEL, D_MODEL)]


def _mod_kernel(ct_ref, w_ref, b_ref, o_ref):
    ct = ct_ref[...]
    s = _silu(ct)
    w = w_ref[0]
    b = b_ref[0]
    o_ref[0] = jnp.zeros(o_ref.shape[1:], F32)
    for r in range(1 + DEC_BATCH):
        o_ref[0, r:r + 1, :] = jnp.sum(s[:, r:r + 1] * w, axis=0, keepdims=True) + b


def _modulation(cond_t, w_mod, b_mod):
    tn = 1024
    n_cols = N_MOD * D_MODEL
    return pl.pallas_call(
        _mod_kernel,
        grid=(DEPTH, n_cols // tn),
        in_specs=[
            pl.BlockSpec((D_MODEL, 8), lambda l, j: (0, 0)),
            pl.BlockSpec((1, D_MODEL, tn), lambda l, j: (l, 0, j)),
            pl.BlockSpec((1, 1, tn), lambda l, j: (l, 0, j)),
        ],
        out_specs=pl.BlockSpec((1, 8, tn), lambda l, j: (l, 0, j)),
        out_shape=jax.ShapeDtypeStruct((DEPTH, 8, n_cols), F32),
        compiler_params=_cparams(("arbitrary", "arbitrary")),
        name="modulation",
    )(cond_t, w_mod, b_mod.reshape(DEPTH, 1, n_cols))


def _ffn_kernel(*refs, tm, k0, l, f, split_in, split_out):
    refs = list(refs)
    x_refs = [refs.pop(0) for _ in range(2 if split_in else 1)]
    mod_ref, gpre_ref, gpost_ref, wi_hbm, wo_hbm = refs[:5]
    n_out = 2 if split_out else 1
    o_refs = refs[5:5 + n_out]
    acc_ref, wi_ref, wo_ref, sg_ref, su_ref, so_ref, sem = refs[5 + n_out:]
    i = pl.program_id(0)
    is_ctx = i < T_CTX // tm
    row = _mod_row(i, tm)
    sh = _mod_vec(mod_ref, row, k0)
    sc = _mod_vec(mod_ref, row, k0 + 1)
    gt = _mod_vec(mod_ref, row, k0 + 2)

    def load_x(rows):
        if split_in:
            return jnp.where(is_ctx, x_refs[0][rows, :], x_refs[1][rows, :])
        return x_refs[0][rows, :]

    def chunk_copies(c):
        lo, slot = c * FF_CHUNK, c % 2
        return (pltpu.make_async_copy(wi_hbm.at[l, f, :, pl.ds(lo, FF_CHUNK)], sg_ref.at[slot], sem.at[0, slot]),
                pltpu.make_async_copy(wi_hbm.at[l, f, :, pl.ds(D_FF + lo, FF_CHUNK)], su_ref.at[slot],
                                      sem.at[1, slot]),
                pltpu.make_async_copy(wo_hbm.at[l, f, pl.ds(lo, FF_CHUNK), :], so_ref.at[slot], sem.at[2, slot]))

    def run(stream_weights):
        if stream_weights:
            for cp in chunk_copies(0):
                cp.start()
        n_parts = 1 if stream_weights else FFN_PARTS
        pm = tm // n_parts
        for p in range(n_parts):
            rows = slice(p * pm, (p + 1) * pm)
            x = load_x(rows)
            h = (_rms(x, gpre_ref[0, 0]) * (1.0 + sc) + sh).astype(BF16)
            for c in range(N_FF_CHUNKS):
                lo = c * FF_CHUNK
                if stream_weights:
                    if c + 1 < N_FF_CHUNKS:
                        for cp in chunk_copies(c + 1):
                            cp.start()
                    for cp in chunk_copies(c):
                        cp.wait()
                    slot = c % 2
                    wi_ref[:, lo:lo + FF_CHUNK] = sg_ref[slot].astype(BF16)
                    wi_ref[:, D_FF + lo:D_FF + lo + FF_CHUNK] = su_ref[slot].astype(BF16)
                    wo_ref[lo:lo + FF_CHUNK, :] = so_ref[slot].astype(BF16)
                g = jnp.dot(h, wi_ref[:, lo:lo + FF_CHUNK], preferred_element_type=F32)
                u = jnp.dot(h, wi_ref[:, D_FF + lo:D_FF + lo + FF_CHUNK], preferred_element_type=F32)
                a = (_silu(g) * u).astype(BF16)
                part = jnp.dot(a, wo_ref[lo:lo + FF_CHUNK, :], preferred_element_type=F32)
                if c == 0:
                    acc_ref[rows, :] = part
                else:
                    acc_ref[rows, :] += part
            out = x + (FFN_RES * gt) * _rms(acc_ref[rows, :], gpost_ref[0, 0])
            if split_out:
                @pl.when(is_ctx)
                def _(rows=rows, out=out):
                    o_refs[0][rows, :] = out

                @pl.when(jnp.logical_not(is_ctx))
                def _(rows=rows, out=out):
                    o_refs[1][rows, :] = out
            else:
                o_refs[0][rows, :] = out

    pl.when(i == 0)(functools.partial(run, True))
    pl.when(i != 0)(functools.partial(run, False))


def _ffn(xs, mod, norm_g, w_ffn_in, w_ffn_out, l, f, split_out=False):
    tm = TM_FFN
    ctx_tiles = T_CTX // tm
    split_in = isinstance(xs, tuple)

    def lf_block(shape, k):
        return pl.BlockSpec((1, 1) + shape, lambda i: (l, k, 0, 0), pipeline_mode=pl.Buffered(1))

    merged = pl.BlockSpec((tm, D_MODEL), lambda i: (i, 0))
    ctx_part = pl.BlockSpec((tm, D_MODEL), lambda i: (jnp.minimum(i, ctx_tiles - 1), 0))
    lat_part = pl.BlockSpec((tm, D_MODEL), lambda i: (jnp.maximum(i - ctx_tiles, 0), 0))
    parts_shape = [jax.ShapeDtypeStruct((T_CTX, D_MODEL), F32), jax.ShapeDtypeStruct((T_LAT, D_MODEL), F32)]
    return pl.pallas_call(
        functools.partial(_ffn_kernel, tm=tm, k0=6 * f, l=l, f=f, split_in=split_in, split_out=split_out),
        grid=(T_ALL // tm,),
        in_specs=([ctx_part, lat_part] if split_in else [merged]) + [
            _layer_block(l, (8, N_MOD * D_MODEL)),
            lf_block((1, D_MODEL), 4 * f),
            lf_block((1, D_MODEL), 4 * f + 1),
            pl.BlockSpec(memory_space=pl.ANY),
            pl.BlockSpec(memory_space=pl.ANY),
        ],
        out_specs=[ctx_part, lat_part] if split_out else merged,
        out_shape=parts_shape if split_out else jax.ShapeDtypeStruct((T_ALL, D_MODEL), F32),
        scratch_shapes=[pltpu.VMEM((tm, D_MODEL), F32),
                        pltpu.VMEM((D_MODEL, 2 * D_FF), BF16),
                        pltpu.VMEM((D_FF, D_MODEL), BF16),
                        pltpu.VMEM((2, D_MODEL, FF_CHUNK), F32),
                        pltpu.VMEM((2, D_MODEL, FF_CHUNK), F32),
                        pltpu.VMEM((2, FF_CHUNK, D_MODEL), F32),
                        pltpu.SemaphoreType.DMA((3, 2))],
        compiler_params=_cparams(("arbitrary",)),
        name="ffn",
    )(*(xs if split_in else (xs,)), mod, norm_g, norm_g, w_ffn_in, w_ffn_out)


def _proj_kernel(x_ref, xp_ref, xn_ref, mod_ref, g_ref, w_ref, dtb_row_ref, dtb_col_ref,
                 scw_ref, scb_ref,
                 z_ref, xs_ref, bc_ref, v_ref, pool_ref, gate_ref, dt_ref, dt_t_ref,
                 wa_ref, wb_ref, wdt_ref, wdt_t_ref, stage_ref, sem, xpad_ref, *, l):
    i = pl.program_id(0)
    lat = i >= CTX_TILES
    k = jnp.maximum(i - CTX_TILES, 0) % LAT_TILES
    has_prev = jnp.logical_and(lat, k != 0)
    has_next = jnp.logical_and(lat, k != LAT_TILES - 1)
    row = _mod_row(i, TILE)
    sh = _mod_vec(mod_ref, row, 3)
    sc = _mod_vec(mod_ref, row, 4)
    x_all = jnp.concatenate([x_ref[...], xp_ref[...], xn_ref[...]], axis=0)
    h_all = (_rms(x_all, g_ref[0, 0]) * (1.0 + sc) + sh).astype(BF16)
    h = h_all[:TILE]

    slabs = ([(j * W_SLAB, W_SLAB, wa_ref, j * W_SLAB) for j in range(N_WA // W_SLAB)]
             + [(OFF_XBC, 2 * SSD_HEADS, None, 0)]
             + [(OFF_DT + j * W_SLAB, W_SLAB, wb_ref, j * W_SLAB) for j in range(N_WB // W_SLAB)])

    def slab_copy(s):
        src, n, _, _ = slabs[s]
        return pltpu.make_async_copy(w_ref.at[l, pl.ds(src, n), :], stage_ref.at[s % 2, pl.ds(0, n), :],
                                     sem.at[s % 2])

    @pl.when(i == 0)
    def _():
        slab_copy(0).start()
        for s, (_, n, dst_ref, dst) in enumerate(slabs):
            if s + 1 < len(slabs):
                slab_copy(s + 1).start()
            slab_copy(s).wait()
            if dst_ref is not None:
                dst_ref[:, dst:dst + n] = stage_ref[s % 2].T.astype(BF16)
            else:
                w_dt = stage_ref[s % 2, 0:2 * SSD_HEADS, :].astype(BF16)
                wdt_t_ref[...] = w_dt
                wdt_ref[...] = jnp.zeros(wdt_ref.shape, BF16)
                for d in range(2):
                    wdt_ref[d * DT_LANES:d * DT_LANES + SSD_HEADS, :] = w_dt[d * SSD_HEADS:(d + 1) * SSD_HEADS]

    def mm_t(a, w_rows):
        return lax.dot_general(a, w_rows, (((1,), (1,)), ((), ())), preferred_element_type=F32)

    def mm(w, lo, hi):
        return jnp.dot(h, w[:, lo:hi], preferred_element_type=F32)

    wa = wa_ref
    xbc = jnp.dot(h_all, wa[:, SSD_D_INNER:N_WA], preferred_element_type=F32)
    xpad_ref[HALO:HALO + TILE, :] = xbc[:TILE]
    xpad_ref[0:HALO, :] = jnp.where(has_prev, xbc[TILE:TILE + HALO], 0.0)
    xpad_ref[HALO + TILE:, :] = jnp.where(has_next, xbc[TILE + HALO:], 0.0)
    def conv_blocks(lo, hi):
        for cb in range(lo, hi):
            col = slice(cb * 128, (cb + 1) * 128)
            for r0 in range(0, TILE, CONV_ROWS):
                first = HALO - SSD_CONV // 2 + r0
                acc = scb_ref[0, :, col]
                for t in range(SSD_CONV):
                    acc = acc + scw_ref[0, t:t + 1, col] * xpad_ref[first + t:first + t + CONV_ROWS, col]
                if cb < SSD_D_INNER // 128:
                    xs_ref[r0:r0 + CONV_ROWS, col] = _silu(acc)
                else:
                    bc_ref[r0:r0 + CONV_ROWS, cb * 128 - SSD_D_INNER:(cb + 1) * 128 - SSD_D_INNER] = (
                        _silu(acc).astype(BF16))

    z_ref[...] = _silu(mm(wa, 0, SSD_D_INNER))
    conv_blocks(0, 4)
    ag = mm(wb_ref, 0, PB_POOL)
    v_ref[...] = ag[:, :CONF_D] * _sigmoid(ag[:, CONF_D:])
    conv_blocks(4, 7)
    pool_ref[...] = mm(wb_ref, PB_POOL, PB_GATE)
    for j in range(3):
        conv_blocks(7 + 3 * j, 10 + 3 * j)
        gate_ref[:, j * D_MODEL:(j + 1) * D_MODEL] = _sigmoid(
            mm(wb_ref, PB_GATE + j * D_MODEL, PB_GATE + (j + 1) * D_MODEL))
    dt_ref[...] = _softplus(mm_t(h, wdt_ref[...]) + dtb_row_ref[0])
    dt_t = lax.dot_general(wdt_t_ref[...], h, (((1,), (1,)), ((), ())), preferred_element_type=F32)
    dt_t_ref[...] = _softplus(dt_t + dtb_col_ref[0])


def _proj(x, mod, norm_g, w_in, dtb_row, dtb_col, scw, scb, l):
    per_tile = TILE // HALO
    n_halo = T_ALL // HALO

    def tile(n):
        return pl.BlockSpec((TILE, n), lambda i: (i, 0))

    outs = ((SSD_D_INNER, F32), (SSD_D_INNER, F32), (BC_COLS, BF16), (CONF_D, F32), (POOL_D, F32),
            (3 * D_MODEL, F32), (2 * DT_LANES, F32))
    return pl.pallas_call(
        functools.partial(_proj_kernel, l=l),
        grid=(N_TILES,),
        in_specs=[
            tile(D_MODEL),
            pl.BlockSpec((HALO, D_MODEL), lambda i: (jnp.maximum(i * per_tile - 1, 0), 0)),
            pl.BlockSpec((HALO, D_MODEL), lambda i: (jnp.minimum((i + 1) * per_tile, n_halo - 1), 0)),
            _layer_block(l, (8, N_MOD * D_MODEL)),
            pl.BlockSpec((1, 1, 1, D_MODEL), lambda i: (l, 2, 0, 0), pipeline_mode=pl.Buffered(1)),
            pl.BlockSpec(memory_space=pl.ANY),
            _layer_block(l, (1, 2 * DT_LANES)),
            _layer_block(l, (2 * SSD_HEADS, 1)),
            _layer_block(l, (SSD_CONV, SSD_XBC)),
            _layer_block(l, (1, SSD_XBC)),
        ],
        out_specs=[tile(n) for n, _ in outs] + [pl.BlockSpec((2 * SSD_HEADS, TILE), lambda i: (0, i))],
        out_shape=[jax.ShapeDtypeStruct((T_ALL, n), dt) for n, dt in outs]
        + [jax.ShapeDtypeStruct((2 * SSD_HEADS, T_ALL), F32)],
        scratch_shapes=[pltpu.VMEM((D_MODEL, N_WA), BF16),
                        pltpu.VMEM((D_MODEL, N_WB), BF16),
                        pltpu.VMEM((2 * DT_LANES, D_MODEL), BF16),
                        pltpu.VMEM((2 * SSD_HEADS, D_MODEL), BF16),
                        pltpu.VMEM((2, W_SLAB, D_MODEL), F32),
                        pltpu.SemaphoreType.DMA((2,)),
                        pltpu.VMEM((TILE + 2 * HALO, SSD_XBC), F32)],
        compiler_params=_cparams(("arbitrary",)),
        name="in_proj",
    )(x, x, x, mod, norm_g, w_in, dtb_row, dtb_col, scw, scb)


_NT = (((1,), (1,)), ((), ()))


def _split3(v):
    p0 = v.astype(BF16)
    r = v - p0.astype(F32)
    p1 = r.astype(BF16)
    p2 = (r - p1.astype(F32)).astype(BF16)
    return p0, p1, p2


def _scan_kernel(*refs, nc, latent, first_layer):
    (xs_ref, bcs_ref, dt_ref, dtt_ref, tri_ref, alr_ref, alc_ref, dsk_ref) = refs[:8]
    refs = refs[8:]
    if latent:
        h0_ref, y_ref = refs[:2]
        refs = refs[2:]
    elif first_layer:
        y_ref, ns_ref = refs[:2]
        refs = refs[2:]
    else:
        y_ref, ns_ref = refs[1:3]
        refs = refs[3:]
    xt_ref, acs_ref, acst_ref, st_ref, s_ref, yt_ref = refs

    def rows_of(c):
        return pl.ds(pl.multiple_of(c * SSD_CHUNK, SSD_CHUNK), SSD_CHUNK)

    def dir_rows(d):
        return slice(d * SSD_HEADS, (d + 1) * SSD_HEADS)

    def b_cols(g):
        return slice(g * SSD_D_STATE, (g + 1) * SSD_D_STATE)

    def c_cols(g):
        lo = SSD_GROUPS * SSD_D_STATE + g * SSD_D_STATE
        return slice(lo, lo + SSD_D_STATE)

    def head_rows(h):
        return slice(h * SSD_HEAD_DIM, (h + 1) * SSD_HEAD_DIM)

    def total_col(acs_t, d):
        return acs_t[:, SSD_CHUNK - 1:] if d == 0 else acs_t[:, :1]

    def cumsums(c, carry):
        rows = rows_of(c)
        for d in range(2):
            tri = tri_ref[d]
            a_row = -jnp.exp(alr_ref[0, d])
            a_col = -jnp.exp(alc_ref[0, d])
            p = jnp.concatenate(_split3(dt_ref[rows, d * DT_LANES:(d + 1) * DT_LANES]), axis=1)
            r = jnp.dot(tri, p, preferred_element_type=F32)
            acs_ref[d, rows, :] = (r[:, :128] + r[:, 128:256] + r[:, 256:]) * a_row
            q = jnp.concatenate(_split3(dtt_ref[dir_rows(d), rows]), axis=0)
            rt = lax.dot_general(q, tri, _NT, preferred_element_type=F32)
            acst_ref[d, :, rows] = (rt[:SSD_HEADS] + rt[SSD_HEADS:2 * SSD_HEADS] + rt[2 * SSD_HEADS:]) * a_col
        return carry

    lax.fori_loop(0, nc, cumsums, 0)

    def local_states(c, carry):
        rows = rows_of(c)
        xt = xs_ref[rows, :].T
        xt_ref[:, rows] = xt
        for d in range(2):
            acs_t = acst_ref[d, :, rows]
            w = dtt_ref[dir_rows(d), rows] * jnp.exp(total_col(acs_t, d) - acs_t)
            for g in range(SSD_GROUPS):
                bg = bcs_ref[rows, b_cols(g)]
                parts = [(xt[head_rows(h), :] * w[h:h + 1, :]).astype(BF16)
                         for h in range(g * HEADS_PER_GROUP, (g + 1) * HEADS_PER_GROUP)]
                st_ref[d, c, g * GROUP_ROWS:(g + 1) * GROUP_ROWS, :] = jnp.dot(
                    jnp.concatenate(parts, axis=0), bg, preferred_element_type=F32)
        return carry

    lax.fori_loop(0, nc, local_states, 0)

    for d in range(2):
        if latent:
            s_ref[...] = h0_ref[0, 0, d]
        else:
            s_ref[...] = jnp.zeros(s_ref.shape, F32)

        def recur(j, carry, d=d):
            c = j if d == 0 else nc - 1 - j
            acs_t = acst_ref[d, :, rows_of(c)]
            e_tot = jnp.broadcast_to(jnp.exp(total_col(acs_t, d)), (SSD_HEADS, SSD_D_STATE))
            for h in range(SSD_HEADS):
                s_old = s_ref[head_rows(h), :]
                cs = st_ref[d, c, head_rows(h), :]
                st_ref[d, c, head_rows(h), :] = s_old
                s_ref[head_rows(h), :] = s_old * e_tot[h:h + 1, :] + cs
            return carry

        lax.fori_loop(0, nc, recur, 0)
        if not latent:
            ns_ref[0, 0, d] = s_ref[...]
    if first_layer and not latent:
        ns_ref[0, 1:] = jnp.zeros((DEPTH - 1, 2, SSD_D_INNER, SSD_D_STATE), F32)

    s_idx = lax.broadcasted_iota(jnp.int32, (SSD_CHUNK, SSD_CHUNK), 0)
    l_idx = lax.broadcasted_iota(jnp.int32, (SSD_CHUNK, SSD_CHUNK), 1)
    visible = (s_idx <= l_idx, s_idx >= l_idx)

    def outputs(c, carry):
        rows = rows_of(c)
        acs_t = [acst_ref[d, :, rows] for d in range(2)]
        e_acs_t = [jnp.exp(a) for a in acs_t]
        dt_t = [dtt_ref[dir_rows(d), rows] for d in range(2)]
        for g in range(SSD_GROUPS):
            bg = bcs_ref[rows, b_cols(g)]
            cg = bcs_ref[rows, c_cols(g)]
            g_t = lax.dot_general(bg, cg, _NT, preferred_element_type=F32)
            y_in = [lax.dot_general(st_ref[d, c, g * GROUP_ROWS:(g + 1) * GROUP_ROWS, :].astype(BF16), cg, _NT,
                                    preferred_element_type=F32) for d in range(2)]
            for hg in range(HEADS_PER_GROUP):
                h = g * HEADS_PER_GROUP + hg
                x_h = xt_ref[head_rows(h), rows]
                lhs, rhs = [], []
                y_h = dsk_ref[0, head_rows(h), :] * x_h
                for d in range(2):
                    seg = jnp.where(visible[d], acs_t[d][h:h + 1, :] - acs_ref[d, rows, h:h + 1], -jnp.inf)
                    rhs.append((g_t * jnp.exp(seg)).astype(BF16))
                    lhs.append((x_h * dt_t[d][h:h + 1, :]).astype(BF16))
                    y_h = y_h + y_in[d][hg * SSD_HEAD_DIM:(hg + 1) * SSD_HEAD_DIM, :] * e_acs_t[d][h:h + 1, :]
                y_h = y_h + jnp.dot(jnp.concatenate(lhs, axis=1), jnp.concatenate(rhs, axis=0),
                                    preferred_element_type=F32)
                yt_ref[head_rows(h), :] = y_h
        y_ref[rows, :] = yt_ref[...].T
        return carry

    lax.fori_loop(0, nc, outputs, 0)


def _scan(xs, bc, dt, dt_t, tri, alr, alc, dskip, state_ssd5, ns_all, l, latent):
    seq = DEC_SEQ if latent else SEQ
    n_seq = DEC_BATCH if latent else BATCH
    first = T_CTX // seq if latent else 0
    nc = seq // SSD_CHUNK
    first_layer = ns_all is None
    in_specs = [
        pl.BlockSpec((seq, SSD_D_INNER), lambda i: (first + i, 0)),
        pl.BlockSpec((seq, BC_COLS), lambda i: (first + i, 0)),
        pl.BlockSpec((seq, 2 * DT_LANES), lambda i: (first + i, 0)),
        pl.BlockSpec((2 * SSD_HEADS, seq), lambda i: (0, first + i)),
        pl.BlockSpec((2, SSD_CHUNK, SSD_CHUNK), lambda i: (0, 0, 0), pipeline_mode=pl.Buffered(1)),
        _layer_block(l, (2, 1, DT_LANES)),
        _layer_block(l, (2, SSD_HEADS, 1)),
        _layer_block(l, (SSD_D_INNER, SSD_D_STATE)),
    ]
    args = (xs, bc, dt, dt_t, tri, alr, alc, dskip)
    scratch = [pltpu.VMEM((SSD_D_INNER, seq), F32),
               pltpu.VMEM((2, seq, DT_LANES), F32),
               pltpu.VMEM((2, SSD_HEADS, seq), F32),
               pltpu.VMEM((2, nc, SSD_D_INNER, SSD_D_STATE), F32),
               pltpu.VMEM((SSD_D_INNER, SSD_D_STATE), F32),
               pltpu.VMEM((SSD_D_INNER, SSD_CHUNK), F32)]
    kern = functools.partial(_scan_kernel, nc=nc, latent=latent, first_layer=first_layer)
    y_spec = pl.BlockSpec((seq, SSD_D_INNER), lambda i: (i, 0))
    if latent:
        state_block = (1, 1, 2, SSD_D_INNER, SSD_D_STATE)
        return pl.pallas_call(
            kern, grid=(n_seq,),
            in_specs=in_specs + [pl.BlockSpec(state_block, lambda i: (i, l, 0, 0, 0))],
            out_specs=y_spec,
            out_shape=jax.ShapeDtypeStruct((T_LAT, SSD_D_INNER), F32),
            scratch_shapes=scratch,
            compiler_params=_cparams(("arbitrary",)),
            name="ssd_scan_latent",
        )(*args, state_ssd5)
    ns_shape = jax.ShapeDtypeStruct((BATCH, DEPTH, 2, SSD_D_INNER, SSD_D_STATE), F32)
    y_shape = jax.ShapeDtypeStruct((T_CTX, SSD_D_INNER), F32)
    if first_layer:
        return pl.pallas_call(
            kern, grid=(n_seq,),
            in_specs=in_specs,
            out_specs=[y_spec, pl.BlockSpec((1, DEPTH, 2, SSD_D_INNER, SSD_D_STATE), lambda i: (i, 0, 0, 0, 0))],
            out_shape=[y_shape, ns_shape],
            scratch_shapes=scratch,
            compiler_params=_cparams(("arbitrary",)),
            name="ssd_scan_ctx",
        )(*args)
    return pl.pallas_call(
        kern, grid=(n_seq,),
        in_specs=in_specs + [pl.BlockSpec(memory_space=pl.ANY)],
        out_specs=[y_spec, pl.BlockSpec((1, 1, 2, SSD_D_INNER, SSD_D_STATE), lambda i: (i, l, 0, 0, 0))],
        out_shape=[y_shape, ns_shape],
        scratch_shapes=scratch,
        input_output_aliases={len(args): 1},
        compiler_params=_cparams(("arbitrary",)),
        name="ssd_scan_ctx",
    )(*args, ns_all)


CONF_SHIFT_ROWS = TILE + 3 * 8
CONF_ROWS = 64


def _fill_padded(pad_ref, cur_ref, prev_ref, next_ref, has_prev, has_next):
    pad_ref[0:HALO, :] = jnp.where(has_prev, prev_ref[...], 0.0)
    pad_ref[HALO:HALO + TILE, :] = cur_ref[...]
    pad_ref[HALO + TILE:, :] = jnp.where(has_next, next_ref[...], 0.0)


def _merge_kernel(x_ref, mod_ref, yc_ref, yl_ref, z_ref, gate_ref,
                  vc_ref, vp_ref, vn_ref, pc_ref, pp_ref, pn_ref,
                  ccw_ref, ccb_ref, lng_ref, lnb_ref, wbc_ref, pw_ref, psc_ref, wbp_ref,
                  sng_ref, wbs_ref, wout_ref, gpost_ref,
                  o_ref,
                  vpad_ref, vsh_ref, ppad_ref, cv_ref, mix_ref):
    i = pl.program_id(0)
    lat = i >= CTX_TILES
    k = jnp.maximum(i - CTX_TILES, 0) % LAT_TILES
    has_prev = jnp.logical_and(lat, k != 0)
    has_next = jnp.logical_and(lat, k != LAT_TILES - 1)
    _fill_padded(vpad_ref, vc_ref, vp_ref, vn_ref, has_prev, has_next)
    _fill_padded(ppad_ref, pc_ref, pp_ref, pn_ref, has_prev, has_next)

    y = jnp.where(lat, yl_ref[...], yc_ref[...])
    br_ssd = jnp.dot(_rms(y * z_ref[...], sng_ref[0]).astype(BF16), wbs_ref[0], preferred_element_type=F32)

    for s in range(1, 8):
        vsh_ref[s - 1] = vpad_ref[s:s + CONF_SHIFT_ROWS, :]

    for cb in range(CONF_D // 128):
        col = slice(cb * 128, (cb + 1) * 128)
        first = HALO - CONF_KERNEL // 2
        for r0 in range(0, TILE, CONF_ROWS):
            acc = ccb_ref[0, :, col]
            for t in range(CONF_KERNEL):
                off = first + t
                a, s = off // 8, off % 8
                src = vpad_ref if s == 0 else vsh_ref.at[s - 1]
                acc = acc + ccw_ref[0, t:t + 1, col] * src[8 * a + r0:8 * a + r0 + CONF_ROWS, col]
            cv_ref[r0:r0 + CONF_ROWS, col] = acc
    cv = cv_ref[...]
    mu = jnp.mean(cv, axis=-1, keepdims=True)
    cen = cv - mu
    var = jnp.mean(cen * cen, axis=-1, keepdims=True)
    ln = cen * lax.rsqrt(var + EPS) * lng_ref[0] + lnb_ref[0]
    br_conf = jnp.dot(_silu(ln).astype(BF16), wbc_ref[0], preferred_element_type=F32)

    seq_len = jnp.where(lat, DEC_SEQ, SEQ)
    pos = k * TILE + lax.broadcasted_iota(jnp.int32, (TILE, 1), 0)
    for gi, w in enumerate(POOL_WINDOWS):
        cols = slice(gi * 128, (gi + 1) * 128)
        first = HALO - w // 2
        s = ppad_ref[first:first + TILE, cols]
        for j in range(1, w):
            s = s + ppad_ref[first + j:first + j + TILE, cols]
        lo = jnp.maximum(pos - w // 2, 0)
        hi = jnp.minimum(pos - w // 2 + w, seq_len)
        pooled = s / (hi - lo).astype(F32) - pc_ref[:, cols]
        mixed = jnp.dot(pooled.astype(BF16), pw_ref[0, gi], preferred_element_type=F32)
        mix_ref[:, cols] = mixed * psc_ref[0, :, cols]
    br_pool = jnp.dot(mix_ref[...].astype(BF16), wbp_ref[0], preferred_element_type=F32)

    merged =(gate_ref[:, 0:D_MODEL] * br_ssd
              + gate_ref[:, D_MODEL:2 * D_MODEL] * br_conf
              + gate_ref[:, 2 * D_MODEL:] * br_pool)
    yo = jnp.dot(merged.astype(BF16), wout_ref[0], preferred_element_type=F32)
    gt = _mod_vec(mod_ref, _mod_row(i, TILE), 5)
    o_ref[...] = x_ref[...] + gt * _rms(yo, gpost_ref[0, 0])


def _merge(x, mod, y_ctx, y_lat, z, gate, v, pool, ccw, ccb, lng, lnb, wbc, pw, psc, wbp, sng, wbs, wout,
           norm_g, l):
    per_tile = TILE // HALO
    n_halo = T_ALL // HALO

    def cur(n):
        return pl.BlockSpec((TILE, n), lambda i: (i, 0))

    def trio(n):
        return [cur(n),
                pl.BlockSpec((HALO, n), lambda i: (jnp.maximum(i * per_tile - 1, 0), 0)),
                pl.BlockSpec((HALO, n), lambda i: (jnp.minimum((i + 1) * per_tile, n_halo - 1), 0))]

    return pl.pallas_call(
        _merge_kernel,
        grid=(N_TILES,),
        in_specs=[
            cur(D_MODEL),
            _layer_block(l, (8, N_MOD * D_MODEL)),
            pl.BlockSpec((TILE, SSD_D_INNER), lambda i: (jnp.minimum(i, CTX_TILES - 1), 0)),
            pl.BlockSpec((TILE, SSD_D_INNER), lambda i: (jnp.maximum(i - CTX_TILES, 0), 0)),
            cur(SSD_D_INNER), cur(3 * D_MODEL)]
        + trio(CONF_D) + trio(POOL_D) + [
            _layer_block(l, (CONF_KERNEL, CONF_D)), _layer_block(l, (1, CONF_D)),
            _layer_block(l, (1, CONF_D)), _layer_block(l, (1, CONF_D)),
            _layer_block(l, (CONF_D, D_MODEL)),
            _layer_block(l, (len(POOL_WINDOWS), 128, 128)), _layer_block(l, (1, POOL_D)),
            _layer_block(l, (POOL_D, D_MODEL)),
            _layer_block(l, (1, SSD_D_INNER)),
            _layer_block(l, (SSD_D_INNER, D_MODEL)),
            _layer_block(l, (D_MODEL, D_MODEL)),
            pl.BlockSpec((1, 1, 1, D_MODEL), lambda i: (l, 3, 0, 0), pipeline_mode=pl.Buffered(1)),
        ],
        out_specs=cur(D_MODEL),
        out_shape=jax.ShapeDtypeStruct((T_ALL, D_MODEL), F32),
        scratch_shapes=[pltpu.VMEM((TILE + 2 * HALO, CONF_D), F32),
                        pltpu.VMEM((7, CONF_SHIFT_ROWS, CONF_D), F32),
                        pltpu.VMEM((TILE + 2 * HALO, POOL_D), F32),
                        pltpu.VMEM((TILE, CONF_D), F32),
                        pltpu.VMEM((TILE, POOL_D), F32)],
        compiler_params=_cparams(("arbitrary",)),
        name="mixer_merge",
    )(x, mod, y_ctx, y_lat, z, gate, v, v, v, pool, pool, pool,
      ccw, ccb, lng, lnb, wbc, pw, psc, wbp, sng, wbs, wout, norm_g)


def _pos_embed_2d(n_tokens):
    rows = n_tokens // GRID_W
    r, col = jnp.meshgrid(jnp.arange(rows), jnp.arange(GRID_W), indexing='ij')
    r = r.reshape(-1).astype(F32)
    col = col.reshape(-1).astype(F32)
    q = D_MODEL // 4
    omega = 1.0 / (10000.0 ** (jnp.arange(q, dtype=F32) / q))
    ar = r[:, None] * omega
    ac = col[:, None] * omega
    return jnp.concatenate([jnp.sin(ar), jnp.cos(ar), jnp.sin(ac), jnp.cos(ac)], axis=-1)


def _pad_lanes(a, n):
    return jnp.pad(a, [(0, 0)] * (a.ndim - 1) + [(0, n - a.shape[-1])])


def kernel(x_prompt, x_sample, state_ssd, c, c_ctx, w_mod, b_mod, norm_g, w_ffn_in, w_ffn_out, w_in,
           ssd_conv_w, ssd_conv_b, ssd_a_log, ssd_dt_bias, ssd_d, ssd_norm_g, w_br_ssd, conf_conv_w,
           conf_conv_b, conf_ln_g, conf_ln_b, w_br_conf, pool_w, pool_scale, w_br_pool, w_out):
    xs = x_sample + _pos_embed_2d(DEC_SEQ).astype(x_sample.dtype)[None]
    x = (x_prompt.reshape(T_CTX, D_MODEL), xs.reshape(T_LAT, D_MODEL))

    cond_t = jnp.concatenate([c_ctx[None, :], c, jnp.zeros((8 - 1 - DEC_BATCH, D_MODEL), F32)], axis=0).T
    mod = _modulation(cond_t, w_mod, b_mod)

    w_in_t = jnp.swapaxes(w_in, 1, 2)
    dtb_row = _pad_lanes(ssd_dt_bias, DT_LANES).reshape(DEPTH, 1, 2 * DT_LANES)
    dtb_col = ssd_dt_bias.reshape(DEPTH, 2 * SSD_HEADS, 1)
    wbs_b = w_br_ssd.astype(BF16)
    wbc_b = w_br_conf.astype(BF16)
    wbp_b = w_br_pool.astype(BF16)
    wout_b = w_out.astype(BF16)
    pw_b = pool_w.astype(BF16)
    ng = norm_g.reshape(DEPTH, 6, 1, D_MODEL)
    alr = _pad_lanes(ssd_a_log, DT_LANES).reshape(DEPTH, 2, 1, DT_LANES)
    alc = ssd_a_log.reshape(DEPTH, 2, SSD_HEADS, 1)
    dskip = jnp.broadcast_to(jnp.repeat(ssd_d, SSD_HEAD_DIM, axis=1)[:, :, None],
                             (DEPTH, SSD_D_INNER, SSD_D_STATE))
    idx = jnp.arange(SSD_CHUNK)
    tri = jnp.stack([idx[:, None] >= idx[None, :], idx[:, None] <= idx[None, :]]).astype(BF16)
    state5 = state_ssd.reshape(DEC_BATCH, DEPTH, 2, SSD_D_INNER, SSD_D_STATE)
    ns_all = None

    def row1(a):
        return a.reshape(DEPTH, 1, a.shape[-1])

    for l in range(DEPTH):
        x = _ffn(x, mod, ng, w_ffn_in, w_ffn_out, l, 0)
        z, xs_c, bc_c, v, pool, gate, dt, dt_t = _proj(x, mod, ng, w_in_t, dtb_row, dtb_col,
                                                       ssd_conv_w, row1(ssd_conv_b), l)
        scan_args = (xs_c, bc_c, dt, dt_t, tri, alr, alc, dskip)
        y_ctx, ns_all = _scan(*scan_args, None, ns_all, l, latent=False)
        y_lat = _scan(*scan_args, state5, None, l, latent=True)
        x = _merge(x, mod, y_ctx, y_lat, z, gate, v, pool, conf_conv_w, row1(conf_conv_b),
                   row1(conf_ln_g), row1(conf_ln_b), wbc_b, pw_b, row1(pool_scale), wbp_b,
                   row1(ssd_norm_g), wbs_b, wout_b, ng, l)
        x = _ffn(x, mod, ng, w_ffn_in, w_ffn_out, l, 1, split_out=(l == DEPTH - 1))

    y_prompt = x[0].reshape(BATCH, SEQ, D_MODEL)
    y_sample = x[1].reshape(DEC_BATCH, DEC_SEQ, D_MODEL)
    new_state = ns_all.reshape(BATCH, DEPTH, 2, SSD_HEADS, SSD_HEAD_DIM, SSD_D_STATE).astype(x_prompt.dtype)
    return (y_prompt, y_sample, new_state)
```
